```python
import jax, jax.numpy as jnp
from jax import lax
import numpy as np

D_MODEL = 2048
BATCH = 8
SEQ = 2048
DEPTH = 1
DEC_BATCH = 2
DEC_SEQ = 4096
PAST_LEN = 128

HEAD_DIM = 128
N_ATTN_HEADS = 8
ATTN_WIDTH = N_ATTN_HEADS * HEAD_DIM
N_CONV_GROUPS = 8
CONV_WIDTH = D_MODEL - ATTN_WIDTH
MIX_WIDTH = ATTN_WIDTH + CONV_WIDTH
CONV_K = 3
DILATED_PATTERNS = ((128, 1), (512, 4), (2048, 16))
BAND_BLOCK = 64
ROPE_THETA = 500000.0
ROPE_DIM = HEAD_DIM // 4
N_EXPERTS = 32
TOP_K = 4
D_FF = D_MODEL
SWIGLU_LIMIT = 7.0
SWIGLU_ALPHA = 1.702
MOE_BLOCK = 256
EPS = 1e-5

kernel_name = 'hymba_dilated_conv_moe_encoder'


def group_rmsnorm(x, g, n_groups):
    shp = x.shape
    xf = x.astype(jnp.float32).reshape(shp[:-1] + (n_groups, shp[-1] // n_groups))
    y = xf * lax.rsqrt(jnp.mean(xf * xf, axis=-1, keepdims=True) + EPS)
    return (y.reshape(shp) * g.astype(jnp.float32)).astype(x.dtype)


def apply_partial_rope(x):
    S = x.shape[1]
    half = ROPE_DIM // 2
    inv_freq = ROPE_THETA ** (-jnp.arange(half, dtype=jnp.float32) * 2.0 / ROPE_DIM)
    ang = jnp.arange(S, dtype=jnp.float32)[:, None] * inv_freq[None, :]
    cos = jnp.cos(ang)[None, :, None, :]
    sin = jnp.sin(ang)[None, :, None, :]
    xr = x[..., :ROPE_DIM].astype(jnp.float32)
    x1, x2 = xr[..., :half], xr[..., half:]
    rot = jnp.concatenate([x1 * cos - x2 * sin, x2 * cos + x1 * sin], axis=-1).astype(x.dtype)
    return jnp.concatenate([rot, x[..., ROPE_DIM:]], axis=-1)


def dilated_band_attention(q, k, v, window, dilation):
    B, S, H, dh = q.shape
    half = window // (2 * dilation)
    blk = BAND_BLOCK
    L = S // dilation
    nb = -(-L // blk)
    Lp = nb * blk

    def to_classes(t):
        return t.reshape(B, L, dilation, H, dh).transpose(0, 2, 3, 1, 4)

    qb = jnp.pad(to_classes(q), ((0, 0), (0, 0), (0, 0), (0, Lp - L), (0, 0))).reshape(B, dilation, H, nb, blk, dh)

    def windows(t):
        tp = jnp.pad(to_classes(t), ((0, 0), (0, 0), (0, 0), (blk, Lp - L + blk), (0, 0)))
        tp = tp.reshape(B, dilation, H, nb + 2, blk, dh)
        return jnp.concatenate([tp[:, :, :, 0:nb], tp[:, :, :, 1:nb + 1], tp[:, :, :, 2:nb + 2]], axis=4)

    kw = windows(k)
    vw = windows(v)
    s = jnp.einsum('bdhnqc,bdhnkc->bdhnqk', qb, kw, preferred_element_type=jnp.float32)
    qpos = jnp.arange(nb)[:, None, None] * blk + jnp.arange(blk)[None, :, None]
    kpos = jnp.arange(nb)[:, None, None] * blk - blk + jnp.arange(3 * blk)[None, None, :]
    valid = (jnp.abs(kpos - qpos) <= half) & (kpos >= 0) & (kpos < L)
    s = jnp.where(valid, s, -1e30)
    m = jnp.max(s, axis=-1, keepdims=True)
    p = jnp.exp(s - m)
    l = jnp.sum(p, axis=-1, keepdims=True)
    acc = jnp.einsum('bdhnqk,bdhnkc->bdhnqc', p, vw.astype(jnp.float32))
    o = acc / l
    lse = (m + jnp.log(l))[..., 0]
    o = o.reshape(B, dilation, H, Lp, dh)[:, :, :, :L].transpose(0, 3, 1, 2, 4).reshape(B, S, H, dh)
    lse = lse.reshape(B, dilation, H, Lp)[:, :, :, :L].transpose(0, 3, 1, 2).reshape(B, S, H)
    return o, lse


def longnet_attention(q, k, v):
    outs, lses = [], []
    for window, dilation in DILATED_PATTERNS:
        o, lse = dilated_band_attention(q, k, v, window, dilation)
        outs.append(o)
        lses.append(lse)
    wts = jax.nn.softmax(jnp.stack(lses, axis=0), axis=0)
    return jnp.einsum('gbsh,gbshc->bshc', wts, jnp.stack(outs, axis=0))


def gated_short_conv(xin, b_gate, c_gate, conv_w):
    u = c_gate * xin
    up = jnp.pad(u, ((0, 0), (1, 1), (0, 0)))
    y = conv_w[0] * up[:, :-2] + conv_w[1] * up[:, 1:-1] + conv_w[2] * up[:, 2:]
    return b_gate * y


def parallel_mixer(h, w_in, conv_w, attn_norm_g, conv_norm_g, w_out):
    B, S, _ = h.shape
    A, C = ATTN_WIDTH, CONV_WIDTH
    proj = h @ w_in
    q, k, v, cb, cc, cx = jnp.split(proj, [A, 2 * A, 3 * A, 3 * A + C, 3 * A + 2 * C], axis=-1)
    q = apply_partial_rope(q.reshape(B, S, N_ATTN_HEADS, HEAD_DIM)) * (HEAD_DIM ** -0.5)
    k = apply_partial_rope(k.reshape(B, S, N_ATTN_HEADS, HEAD_DIM))
    v = v.reshape(B, S, N_ATTN_HEADS, HEAD_DIM)
    attn = longnet_attention(q, k, v).reshape(B, S, A).astype(h.dtype)
    conv = gated_short_conv(cx, cb, cc, conv_w)
    mixed = jnp.concatenate([group_rmsnorm(attn, attn_norm_g, N_ATTN_HEADS),
                             group_rmsnorm(conv, conv_norm_g, N_CONV_GROUPS)], axis=-1)
    return mixed @ w_out


def expert_block(xb, wg, bg, wu, bu, wd, bd):
    g = jnp.minimum(xb @ wg + bg, SWIGLU_LIMIT)
    u = jnp.clip(xb @ wu + bu, -SWIGLU_LIMIT, SWIGLU_LIMIT)
    return ((u + 1.0) * (g * jax.nn.sigmoid(SWIGLU_ALPHA * g))) @ wd + bd


def moe(h, w_router, b_router, w_gate, b_gate, w_up, b_up, w_down, b_down):
    B, S, D = h.shape
    x = h.reshape(-1, D)
    T = x.shape[0]
    logits = jnp.dot(x, w_router, preferred_element_type=jnp.float32) + b_router.astype(jnp.float32)
    top_vals, top_idx = lax.top_k(logits, TOP_K)
    gates = jax.nn.softmax(top_vals, axis=-1)
    n = T * TOP_K
    expert_flat = top_idx.reshape(-1)
    token_flat = jnp.arange(n) // TOP_K
    order = jnp.argsort(expert_flat)
    e_sorted = expert_flat[order]
    tok_sorted = token_flat[order]
    gate_sorted = gates.reshape(-1)[order]
    counts = jnp.bincount(expert_flat, length=N_EXPERTS)
    starts = jnp.cumsum(counts) - counts
    padded = (counts + MOE_BLOCK - 1) // MOE_BLOCK * MOE_BLOCK
    pad_ends = jnp.cumsum(padded)
    pad_starts = pad_ends - padded
    dest = pad_starts[e_sorted] + jnp.arange(n) - starts[e_sorted]
    n_blocks = -(-n // MOE_BLOCK) + N_EXPERTS
    P = n_blocks * MOE_BLOCK
    buf = jnp.zeros((P, D), x.dtype).at[dest].set(x[tok_sorted])
    block_expert = jnp.minimum(jnp.searchsorted(pad_ends, jnp.arange(n_blocks) * MOE_BLOCK, side='right'), N_EXPERTS - 1)

    def run_block(args):
        xb, e = args
        return expert_block(xb, w_gate[e], b_gate[e], w_up[e], b_up[e], w_down[e], b_down[e])

    y_buf = lax.map(run_block, (buf.reshape(n_blocks, MOE_BLOCK, D), block_expert)).reshape(P, D)
    y = y_buf[dest] * gate_sorted[:, None].astype(x.dtype)
    out = jax.ops.segment_sum(y, tok_sorted, num_segments=T)
    return out.reshape(B, S, D)


def trunk(x, norm1_g, w_in, conv_w, attn_norm_g, conv_norm_g, w_out, norm2_g,
          w_router, b_router, w_gate, b_gate, w_up, b_up, w_down, b_down, final_norm_g):
    for l in range(DEPTH):
        x = x + parallel_mixer(group_rmsnorm(x, norm1_g[l], 1), w_in[l], conv_w[l],
                               attn_norm_g[l], conv_norm_g[l], w_out[l])
        x = x + moe(group_rmsnorm(x, norm2_g[l], 1), w_router[l], b_router[l], w_gate[l], b_gate[l],
                    w_up[l], b_up[l], w_down[l], b_down[l])
    return group_rmsnorm(x, final_norm_g, 1)


def setup_inputs(seed: int = 0) -> dict:
    key = jax.random.key(seed)
    ks = jax.random.split(key, 20)
    f32 = jnp.float32
    nrm = lambda k, shape, s: jax.random.normal(k, shape, f32) * s
    return {
        'x_prompt': nrm(ks[0], (BATCH, SEQ, D_MODEL), 1.0),
        'x_sample': nrm(ks[1], (DEC_BATCH, DEC_SEQ, D_MODEL), 1.0),
        'norm1_g': 1.0 + nrm(ks[2], (DEPTH, D_MODEL), 0.02),
        'w_in': nrm(ks[3], (DEPTH, D_MODEL, 3 * ATTN_WIDTH + 3 * CONV_WIDTH), D_MODEL ** -0.5),
        'conv_w': nrm(ks[4], (DEPTH, CONV_K, CONV_WIDTH), CONV_K ** -0.5),
        'attn_norm_g': 1.0 + nrm(ks[5], (DEPTH, ATTN_WIDTH), 0.02),
        'conv_norm_g': 1.0 + nrm(ks[6], (DEPTH, CONV_WIDTH), 0.02),
        'w_out': nrm(ks[7], (DEPTH, MIX_WIDTH, D_MODEL), MIX_WIDTH ** -0.5),
        'norm2_g': 1.0 + nrm(ks[8], (DEPTH, D_MODEL), 0.02),
        'w_router': nrm(ks[9], (DEPTH, D_MODEL, N_EXPERTS), D_MODEL ** -0.5),
        'b_router': nrm(ks[10], (DEPTH, N_EXPERTS), 0.01),
        'w_gate': nrm(ks[11], (DEPTH, N_EXPERTS, D_MODEL, D_FF), D_MODEL ** -0.5),
        'b_gate': nrm(ks[12], (DEPTH, N_EXPERTS, D_FF), 0.01),
        'w_up': nrm(ks[13], (DEPTH, N_EXPERTS, D_MODEL, D_FF), D_MODEL ** -0.5),
        'b_up': nrm(ks[14], (DEPTH, N_EXPERTS, D_FF), 0.01),
        'w_down': nrm(ks[15], (DEPTH, N_EXPERTS, D_FF, D_MODEL), D_FF ** -0.5),
        'b_down': nrm(ks[16], (DEPTH, N_EXPERTS, D_MODEL), 0.01),
        'final_norm_g': 1.0 + nrm(ks[17], (D_MODEL,), 0.02),
    }


def reference(x_prompt, x_sample, norm1_g, w_in, conv_w, attn_norm_g, conv_norm_g, w_out, norm2_g,
              w_router, b_router, w_gate, b_gate, w_up, b_up, w_down, b_down, final_norm_g):
    y_prompt = trunk(x_prompt, norm1_g, w_in, conv_w, attn_norm_g, conv_norm_g, w_out, norm2_g,
                     w_router, b_router, w_gate, b_gate, w_up, b_up, w_down, b_down, final_norm_g)
    y_sample = trunk(x_sample, norm1_g, w_in, conv_w, attn_norm_g, conv_norm_g, w_out, norm2_g,
                     w_router, b_router, w_gate, b_gate, w_up, b_up, w_down, b_down, final_norm_g)
    return (y_prompt, y_sample)
```

```python
import functools

import jax
import jax.numpy as jnp
from jax import lax
from jax.experimental import pallas as pl
from jax.experimental.pallas import tpu as pltpu

F32 = jnp.float32
BF16 = jnp.bfloat16

D_MODEL = 2048
HEAD_DIM = 128
N_HEADS = 8
ATTN_WIDTH = N_HEADS * HEAD_DIM
CONV_WIDTH = D_MODEL - ATTN_WIDTH
N_CONV_GROUPS = 8
PROJ_WIDTH = 3 * ATTN_WIDTH + 3 * CONV_WIDTH
DILATED_PATTERNS = ((128, 1), (512, 4), (2048, 16))
BAND_BLOCK = 64
ROPE_THETA = 500000.0
ROPE_DIM = HEAD_DIM // 4
ROPE_HALF = ROPE_DIM // 2
N_EXPERTS = 32
TOP_K = 4
D_FF = D_MODEL
SWIGLU_LIMIT = 7.0
SWIGLU_ALPHA = 1.702
EPS = 1e-5

LANES = 128
BF16_SUBLANES = 16
VMEM_LIMIT = 56 * 1024 * 1024

IN_TM = 512
IN_TN = 1024
MIX_TM = 256
MOE_SUB = 256
MOE_TILE = 1024
MOE_NSUB = MOE_TILE // MOE_SUB
MOE_TF = 256
FIN_TM = 256
ATTN_ROWS_X_HEADS = 8192


def _params(sem):
    return pltpu.CompilerParams(dimension_semantics=sem, vmem_limit_bytes=VMEM_LIMIT)


def _inproj_body(x_ref, g_ref, w_ref, cos_ref, sa_ref, sb_ref, o_ref, h_scr):
    j = pl.program_id(1)

    @pl.when(j == 0)
    def _():
        x = x_ref[...]
        ms = jnp.mean(x * x, axis=-1, keepdims=True)
        h_scr[...] = (x * lax.rsqrt(ms + EPS) * g_ref[...]).astype(BF16)

    acc = jnp.dot(h_scr[...], w_ref[...], preferred_element_type=F32)

    @pl.when(j < 2)
    def _():
        scale = jnp.where(j == 0, HEAD_DIM ** -0.5, 1.0).astype(F32)
        c, sa, sb = cos_ref[...], sa_ref[...], sb_ref[...]
        for h in range(N_HEADS):
            sl = slice(h * HEAD_DIM, (h + 1) * HEAD_DIM)
            a = acc[:, sl]
            r = (a * c + pltpu.roll(a, HEAD_DIM - ROPE_HALF, 1) * sa
                 + pltpu.roll(a, ROPE_HALF, 1) * sb)
            o_ref[:, sl] = (r * scale).astype(BF16)

    @pl.when(j >= 2)
    def _():
        o_ref[...] = acc.astype(BF16)


def _rope_tables(S):
    inv_freq = ROPE_THETA ** (-jnp.arange(ROPE_HALF, dtype=F32) * 2.0 / ROPE_DIM)
    ang = jnp.arange(S, dtype=F32)[:, None] * inv_freq[None, :]
    cos, sin = jnp.cos(ang), jnp.sin(ang)
    pad = jnp.zeros((S, HEAD_DIM - ROPE_DIM), F32)
    zero = jnp.zeros((S, ROPE_HALF), F32)
    c = jnp.concatenate([cos, cos, pad + 1.0], axis=1)
    sa = jnp.concatenate([-sin, zero, pad], axis=1)
    sb = jnp.concatenate([zero, sin, pad], axis=1)
    return c, sa, sb


def _in_projection(x2d, S, gain, w_bf16, tables):
    T = x2d.shape[0]
    tm = IN_TM
    pos_blocks = S // tm
    tab_spec = pl.BlockSpec((tm, HEAD_DIM), lambda i, j: (i % pos_blocks, 0))
    return pl.pallas_call(
        _inproj_body,
        grid=(T // tm, PROJ_WIDTH // IN_TN),
        in_specs=[
            pl.BlockSpec((tm, D_MODEL), lambda i, j: (i, 0)),
            pl.BlockSpec((1, D_MODEL), lambda i, j: (0, 0)),
            pl.BlockSpec((D_MODEL, IN_TN), lambda i, j: (0, j)),
            tab_spec, tab_spec, tab_spec,
        ],
        out_specs=pl.BlockSpec((tm, IN_TN), lambda i, j: (i, j)),
        out_shape=jax.ShapeDtypeStruct((T, PROJ_WIDTH), BF16),
        scratch_shapes=[pltpu.VMEM((tm, D_MODEL), BF16)],
        compiler_params=_params(("arbitrary", "arbitrary")),
        name="in_projection",
    )(x2d, gain, w_bf16, *tables)


def _band_body(q_ref, k_ref, v_ref, o_ref, lse_ref, *, L, hb, W, half):
    g = pl.program_id(2)
    blk = BAND_BLOCK
    lane = lax.broadcasted_iota(jnp.int32, (blk, LANES), 1)
    qi = lax.broadcasted_iota(jnp.int32, (blk, W), 0)
    ki = lax.broadcasted_iota(jnp.int32, (blk, W), 1)

    @pl.when(g == 0)
    def _():
        lse_ref[...] = jnp.zeros_like(lse_ref)

    def block(n, carry):
        q0 = pl.multiple_of(n * blk, blk)
        ws = pl.multiple_of(jnp.clip(q0 - blk, 0, L - W), blk)
        valid = jnp.abs((ws + ki) - (q0 + qi)) <= half
        for h in range(hb):
            sl = slice(h * HEAD_DIM, (h + 1) * HEAD_DIM)
            q = q_ref[pl.ds(q0, blk), sl]
            k = k_ref[pl.ds(ws, W), sl]
            v = v_ref[pl.ds(ws, W), sl]
            s = lax.dot_general(q, k, (((1,), (1,)), ((), ())), preferred_element_type=F32)
            s = jnp.where(valid, s, -1e30)
            m = jnp.max(s, axis=-1, keepdims=True)
            p = jnp.exp(s - m)
            l = jnp.sum(p, axis=-1, keepdims=True)
            acc = jnp.dot(p.astype(BF16), v, preferred_element_type=F32)
            o_ref[pl.ds(q0, blk), sl] = (acc / l).astype(BF16)
            lse = m + jnp.log(l)
            old = lse_ref[pl.ds(q0, blk), :]
            lse_ref[pl.ds(q0, blk), :] = jnp.where(lane == g * hb + h, lse, old)
        return carry

    lax.fori_loop(0, L // blk, block, 0)


def _band_attention(proj, B, S, window, dilation):
    d = dilation
    L = S // d
    half = window // (2 * d)
    assert half <= BAND_BLOCK and L % BAND_BLOCK == 0
    hb = min(N_HEADS, ATTN_ROWS_X_HEADS // L)
    ng = N_HEADS // hb
    W = min(3 * BAND_BLOCK, L)
    bw = HEAD_DIM * hb
    col_blocks = PROJ_WIDTH // bw
    part_blocks = ATTN_WIDTH // bw
    pv = proj.reshape(B, L, d * PROJ_WIDTH)

    def in_spec(t):
        return pl.BlockSpec((None, L, bw), lambda b, r, g: (b, 0, r * col_blocks + t * part_blocks + g))

    o, lse = pl.pallas_call(
        functools.partial(_band_body, L=L, hb=hb, W=W, half=half),
        grid=(B, d, ng),
        in_specs=[in_spec(0), in_spec(1), in_spec(2)],
        out_specs=[
            pl.BlockSpec((None, L, bw), lambda b, r, g: (b, 0, r * ng + g)),
            pl.BlockSpec((None, L, LANES), lambda b, r, g: (b, 0, r)),
        ],
        out_shape=[
            jax.ShapeDtypeStruct((B, L, d * ATTN_WIDTH), BF16),
            jax.ShapeDtypeStruct((B, L, d * LANES), F32),
        ],
        compiler_params=_params(("arbitrary", "arbitrary", "arbitrary")),
        name=f"band_attention_d{d}",
    )(pv, pv, pv)
    return o.reshape(B * S, ATTN_WIDTH), lse.reshape(B * S, LANES)


def _mix_body(o1_ref, o2_ref, o3_ref, l1_ref, l2_ref, l3_ref, cb_ref, cc_ref, cx_ref,
              ccp_ref, cxp_ref, ccn_ref, cxn_ref, x_ref, wout_ref, convw_ref, ga_ref, gc_ref,
              g2_ref, wrh_ref, wrl_ref, br_ref,
              x1_ref, h2_ref, idx_ref, gate_ref, mixed_scr, *, tm, S):
    i = pl.program_id(0)

    la, lb, lc = l1_ref[...], l2_ref[...], l3_ref[...]
    mx = jnp.maximum(jnp.maximum(la, lb), lc)
    ea, eb, ec = jnp.exp(la - mx), jnp.exp(lb - mx), jnp.exp(lc - mx)
    den = ea + eb + ec
    wa, wb, wc = ea / den, eb / den, ec / den
    for h in range(N_HEADS):
        sl = slice(h * HEAD_DIM, (h + 1) * HEAD_DIM)
        a = (wa[:, h:h + 1] * o1_ref[:, sl].astype(F32)
             + wb[:, h:h + 1] * o2_ref[:, sl].astype(F32)
             + wc[:, h:h + 1] * o3_ref[:, sl].astype(F32))
        ms = jnp.mean(a * a, axis=-1, keepdims=True)
        mixed_scr[:, sl] = (a * lax.rsqrt(ms + EPS) * ga_ref[:, sl]).astype(BF16)

    row = lax.broadcasted_iota(jnp.int32, (tm, LANES), 0)
    r0 = i * tm
    at_start = (r0 % S) == 0
    at_end = ((r0 + tm) % S) == 0
    last = BF16_SUBLANES - 1
    gw = CONV_WIDTH // N_CONV_GROUPS
    for c in range(N_CONV_GROUPS):
        sl = slice(c * gw, (c + 1) * gw)
        u = cc_ref[:, sl].astype(F32) * cx_ref[:, sl].astype(F32)
        up = ccp_ref[last:last + 1, sl].astype(F32) * cxp_ref[last:last + 1, sl].astype(F32)
        un = ccn_ref[0:1, sl].astype(F32) * cxn_ref[0:1, sl].astype(F32)
        up = jnp.where(at_start, 0.0, up)
        un = jnp.where(at_end, 0.0, un)
        u_prev = jnp.where(row == 0, up, pltpu.roll(u, 1, 0))
        u_next = jnp.where(row == tm - 1, un, pltpu.roll(u, tm - 1, 0))
        y = convw_ref[0:1, sl] * u_prev + convw_ref[1:2, sl] * u + convw_ref[2:3, sl] * u_next
        cv = cb_ref[:, sl].astype(F32) * y
        ms = jnp.mean(cv * cv, axis=-1, keepdims=True)
        mixed_scr[:, ATTN_WIDTH + c * gw:ATTN_WIDTH + (c + 1) * gw] = (
            cv * lax.rsqrt(ms + EPS) * gc_ref[:, sl]).astype(BF16)

    x1 = x_ref[...] + jnp.dot(mixed_scr[...], wout_ref[...], preferred_element_type=F32)
    x1_ref[...] = x1
    ms = jnp.mean(x1 * x1, axis=-1, keepdims=True)
    h2 = x1 * lax.rsqrt(ms + EPS) * g2_ref[...]
    hi = h2.astype(BF16)
    h2_ref[...] = hi

    lo = (h2 - hi.astype(F32)).astype(BF16)
    logits = (jnp.dot(hi, wrh_ref[...], preferred_element_type=F32)
              + jnp.dot(lo, wrh_ref[...], preferred_element_type=F32)
              + jnp.dot(hi, wrl_ref[...], preferred_element_type=F32)) + br_ref[...]

    lane = lax.broadcasted_iota(jnp.int32, (tm, LANES), 1)
    lane_f = lane.astype(F32)
    vals, ids = [], []
    for _ in range(TOP_K):
        m = jnp.max(logits, axis=-1, keepdims=True)
        ix = jnp.min(jnp.where(logits == m, lane_f, float(LANES)), axis=-1, keepdims=True)
        vals.append(m)
        ids.append(ix)
        logits = jnp.where(lane_f == ix, -jnp.inf, logits)
    es = [jnp.exp(v - vals[0]) for v in vals]
    den = es[0] + es[1] + es[2] + es[3]
    idx_tile = jnp.zeros((tm, LANES), F32)
    gate_tile = jnp.zeros((tm, LANES), F32)
    for k in range(TOP_K):
        idx_tile = jnp.where(lane == k, ids[k], idx_tile)
        gate_tile = jnp.where(lane == k, es[k] / den, gate_tile)
    idx_ref[...] = idx_tile.astype(jnp.int32)
    gate_ref[...] = gate_tile


def _mixer_out(x2d, S, proj, attn, conv_w, ga, gc, w_out, g2, wr_hi, wr_lo, br):
    T = x2d.shape[0]
    tm = MIX_TM
    (o1, l1), (o2, l2), (o3, l3) = attn
    hr = BF16_SUBLANES
    n_halo = T // hr
    row_tile = lambda w: pl.BlockSpec((tm, w), lambda i: (i, 0))
    proj_tile = lambda c: pl.BlockSpec((tm, CONV_WIDTH), lambda i: (i, c))
    prev_halo = lambda c: pl.BlockSpec((hr, CONV_WIDTH), lambda i: (jnp.maximum(i * (tm // hr) - 1, 0), c))
    next_halo = lambda c: pl.BlockSpec(
        (hr, CONV_WIDTH), lambda i: (jnp.minimum((i + 1) * (tm // hr), n_halo - 1), c))
    const = lambda shape: pl.BlockSpec(shape, lambda i: (0, 0))
    cb_col, cc_col, cx_col = 3, 4, 5
    return pl.pallas_call(
        functools.partial(_mix_body, tm=tm, S=S),
        grid=(T // tm,),
        in_specs=[
            row_tile(ATTN_WIDTH), row_tile(ATTN_WIDTH), row_tile(ATTN_WIDTH),
            row_tile(LANES), row_tile(LANES), row_tile(LANES),
            proj_tile(cb_col), proj_tile(cc_col), proj_tile(cx_col),
            prev_halo(cc_col), prev_halo(cx_col), next_halo(cc_col), next_halo(cx_col),
            row_tile(D_MODEL),
            const((D_MODEL, D_MODEL)),
            const((3, CONV_WIDTH)), const((1, ATTN_WIDTH)), const((1, CONV_WIDTH)), const((1, D_MODEL)),
            const((D_MODEL, LANES)), const((D_MODEL, LANES)), const((1, LANES)),
        ],
        out_specs=[row_tile(D_MODEL), row_tile(D_MODEL), row_tile(LANES), row_tile(LANES)],
        out_shape=[
            jax.ShapeDtypeStruct((T, D_MODEL), F32),
            jax.ShapeDtypeStruct((T, D_MODEL), BF16),
            jax.ShapeDtypeStruct((T, LANES), jnp.int32),
            jax.ShapeDtypeStruct((T, LANES), F32),
        ],
        scratch_shapes=[pltpu.VMEM((tm, D_MODEL), BF16)],
        compiler_params=_params(("arbitrary",)),
        name="mixer_out_router",
    )(o1, o2, o3, l1, l2, l3, proj, proj, proj, proj, proj, proj, proj, x2d, w_out, conv_w,
      ga, gc, g2, wr_hi, wr_lo, br)


def _expert_body(te_ref, tx_ref, ns_ref, x_ref, wg_ref, bg_ref, wu_ref, bu_ref, wd_ref, bd_ref,
                 y_ref, acc_ref):
    i = pl.program_id(0)
    f = pl.program_id(1)
    nf = pl.num_programs(1)
    nsub = ns_ref[i]

    @pl.when(nsub > 0)
    def _():
        wg = wg_ref[...].astype(BF16)
        wu = wu_ref[...].astype(BF16)
        wd = wd_ref[...].astype(BF16)
        for s in range(MOE_NSUB):
            rows = slice(s * MOE_SUB, (s + 1) * MOE_SUB)

            @pl.when(s < nsub)
            def _():
                x = x_ref[rows, :]
                g = jnp.dot(x, wg, preferred_element_type=F32) + bg_ref[...]
                u = jnp.dot(x, wu, preferred_element_type=F32) + bu_ref[...]
                g = jnp.minimum(g, SWIGLU_LIMIT)
                u = jnp.clip(u, -SWIGLU_LIMIT, SWIGLU_LIMIT)
                a = (u + 1.0) * (g * jax.nn.sigmoid(SWIGLU_ALPHA * g))
                dn = jnp.dot(a.astype(BF16), wd, preferred_element_type=F32)

                @pl.when(f == 0)
                def _():
                    acc_ref[rows, :] = dn

                @pl.when(f > 0)
                def _():
                    acc_ref[rows, :] += dn

    @pl.when(f == nf - 1)
    def _():
        for s in range(MOE_NSUB):
            rows = slice(s * MOE_SUB, (s + 1) * MOE_SUB)

            @pl.when(s < nsub)
            def _():
                y_ref[rows, :] = (acc_ref[rows, :] + bd_ref[...]).astype(BF16)

            @pl.when(s >= nsub)
            def _():
                y_ref[rows, :] = jnp.zeros((MOE_SUB, D_MODEL), BF16)


def _expert_ffn(xs, tile_expert, tile_xblk, tile_nsub, w_gate, b_gate, w_up, b_up, w_down, b_down):
    P = xs.shape[0]
    n_tiles = P // MOE_TILE
    nf = D_FF // MOE_TF
    fsel = lambda i, f, ns: jnp.where(ns[i] > 0, f, nf - 1)
    grid_spec = pltpu.PrefetchScalarGridSpec(
        num_scalar_prefetch=3,
        grid=(n_tiles, nf),
        in_specs=[
            pl.BlockSpec((MOE_TILE, D_MODEL), lambda i, f, te, tx, ns: (tx[i], 0)),
            pl.BlockSpec((None, D_MODEL, MOE_TF), lambda i, f, te, tx, ns: (te[i], 0, fsel(i, f, ns))),
            pl.BlockSpec((None, 1, MOE_TF), lambda i, f, te, tx, ns: (te[i], 0, fsel(i, f, ns))),
            pl.BlockSpec((None, D_MODEL, MOE_TF), lambda i, f, te, tx, ns: (te[i], 0, fsel(i, f, ns))),
            pl.BlockSpec((None, 1, MOE_TF), lambda i, f, te, tx, ns: (te[i], 0, fsel(i, f, ns))),
            pl.BlockSpec((None, MOE_TF, D_MODEL), lambda i, f, te, tx, ns: (te[i], fsel(i, f, ns), 0)),
            pl.BlockSpec((None, 1, D_MODEL), lambda i, f, te, tx, ns: (te[i], 0, 0)),
        ],
        out_specs=pl.BlockSpec((MOE_TILE, D_MODEL), lambda i, f, te, tx, ns: (i, 0)),
        scratch_shapes=[pltpu.VMEM((MOE_TILE, D_MODEL), F32)],
    )
    return pl.pallas_call(
        _expert_body,
        grid_spec=grid_spec,
        out_shape=jax.ShapeDtypeStruct((P, D_MODEL), BF16),
        compiler_params=_params(("arbitrary", "arbitrary")),
        name="expert_ffn",
    )(tile_expert, tile_xblk, tile_nsub, xs, w_gate, b_gate.reshape(N_EXPERTS, 1, D_FF),
      w_up, b_up.reshape(N_EXPERTS, 1, D_FF), w_down, b_down.reshape(N_EXPERTS, 1, D_MODEL))


def _final_body(x1_ref, gate_ref, y0_ref, y1_ref, y2_ref, y3_ref, gf_ref, o_ref):
    gates = gate_ref[...]
    acc = x1_ref[...]
    for k, y_ref in enumerate((y0_ref, y1_ref, y2_ref, y3_ref)):
        acc = acc + gates[:, k:k + 1] * y_ref[...].astype(F32)
    ms = jnp.mean(acc * acc, axis=-1, keepdims=True)
    o_ref[...] = acc * lax.rsqrt(ms + EPS) * gf_ref[...]


def _combine_final(x1, gates, ys, gf):
    T = x1.shape[0]
    tm = FIN_TM
    row_tile = lambda w: pl.BlockSpec((tm, w), lambda i: (i, 0))
    return pl.pallas_call(
        _final_body,
        grid=(T // tm,),
        in_specs=[row_tile(D_MODEL), row_tile(LANES)] + [row_tile(D_MODEL)] * TOP_K
        + [pl.BlockSpec((1, D_MODEL), lambda i: (0, 0))],
        out_specs=row_tile(D_MODEL),
        out_shape=jax.ShapeDtypeStruct((T, D_MODEL), F32),
        compiler_params=_params(("arbitrary",)),
        name="combine_final_norm",
    )(x1, gates, *ys, gf)


def _route(idx):
    n = idx.shape[0] * TOP_K
    n_tiles = n // MOE_TILE + N_EXPERTS
    e_flat = idx.reshape(-1)
    onehot = (e_flat[:, None] == jnp.arange(N_EXPERTS, dtype=jnp.int32)[None, :]).astype(jnp.int32)
    csum = jnp.cumsum(onehot, axis=0)
    rank = jnp.take_along_axis(csum, e_flat[:, None], axis=1)[:, 0] - 1
    counts = csum[-1]
    tiles_per_e = (counts + MOE_TILE - 1) // MOE_TILE
    tile_end = jnp.cumsum(tiles_per_e)
    tile_start = tile_end - tiles_per_e
    slot = tile_start[e_flat] * MOE_TILE + rank
    src = jnp.zeros((n_tiles * MOE_TILE,), jnp.int32).at[slot].set(jnp.arange(n, dtype=jnp.int32) // TOP_K)

    t = jnp.arange(n_tiles, dtype=jnp.int32)
    used = t < tile_end[-1]
    last_used = jnp.maximum(tile_end[-1] - 1, 0)
    tt = jnp.where(used, t, last_used)
    te = jnp.minimum(jnp.searchsorted(tile_end, tt, side="right"), N_EXPERTS - 1).astype(jnp.int32)
    rows = jnp.clip(counts[te] - (tt - tile_start[te]) * MOE_TILE, 0, MOE_TILE)
    nsub = jnp.where(used, (rows + MOE_SUB - 1) // MOE_SUB, 0).astype(jnp.int32)
    return slot.reshape(-1, TOP_K), src, te, tt.astype(jnp.int32), nsub


def kernel(x_prompt, x_sample, norm1_g, w_in, conv_w, attn_norm_g, conv_norm_g, w_out, norm2_g,
           w_router, b_router, w_gate, b_gate, w_up, b_up, w_down, b_down, final_norm_g):
    assert norm1_g.shape[0] == 1, "one layer"
    w_in_b = w_in[0].astype(BF16)
    w_out_b = w_out[0].astype(BF16)
    wr = jnp.pad(w_router[0], ((0, 0), (0, LANES - N_EXPERTS)))
    wr_hi = wr.astype(BF16)
    wr_lo = (wr - wr_hi.astype(F32)).astype(BF16)
    br = jnp.pad(b_router[0], (0, LANES - N_EXPERTS), constant_values=-jnp.inf).reshape(1, LANES)
    g1 = norm1_g[0].reshape(1, D_MODEL)
    g2 = norm2_g[0].reshape(1, D_MODEL)
    ga = attn_norm_g[0].reshape(1, ATTN_WIDTH)
    gc = conv_norm_g[0].reshape(1, CONV_WIDTH)
    gf = final_norm_g.reshape(1, D_MODEL)

    x1s, h2s, idxs, gates = [], [], [], []
    for x in (x_prompt, x_sample):
        B, S, _ = x.shape
        x2d = x.reshape(B * S, D_MODEL)
        proj = _in_projection(x2d, S, g1, w_in_b, _rope_tables(S))
        attn = [_band_attention(proj, B, S, w, d) for w, d in DILATED_PATTERNS]
        x1, h2, idx, gate = _mixer_out(x2d, S, proj, attn, conv_w[0], ga, gc, w_out_b, g2, wr_hi, wr_lo, br)
        x1s.append(x1)
        h2s.append(h2)
        idxs.append(idx[:, :TOP_K])
        gates.append(gate)

    h2 = jnp.concatenate(h2s, axis=0)
    slot, src, te, tx, nsub = _route(jnp.concatenate(idxs, axis=0))
    xs = jnp.take(h2, src, axis=0)
    y = _expert_ffn(xs, te, tx, nsub, w_gate[0], b_gate[0], w_up[0], b_up[0], w_down[0], b_down[0])

    outs = []
    t0 = 0
    for x, x1, gate in zip((x_prompt, x_sample), x1s, gates):
        T = x1.shape[0]
        ys = [jnp.take(y, slot[t0:t0 + T, k], axis=0) for k in range(TOP_K)]
        outs.append(_combine_final(x1, gate, ys, gf).reshape(x.shape))
        t0 += T
    return tuple(outs)
```

```python
import functools

import jax
import jax.numpy as jnp
from jax import lax
from jax.experimental import pallas as pl
from jax.experimental.pallas import tpu as pltpu

F32 = jnp.float32
BF16 = jnp.bfloat16

D_MODEL = 2048
HEAD_DIM = 128
N_HEADS = 8
ATTN_WIDTH = N_HEADS * HEAD_DIM
CONV_WIDTH = D_MODEL - ATTN_WIDTH
N_CONV_GROUPS = 8
QKV_WIDTH = 3 * ATTN_WIDTH
GATE_WIDTH = 3 * CONV_WIDTH
DILATED_PATTERNS = ((128, 1), (512, 4), (2048, 16))
ROPE_THETA = 500000.0
ROPE_DIM = HEAD_DIM // 4
ROPE_HALF = ROPE_DIM // 2
N_EXPERTS = 32
TOP_K = 4
D_FF = D_MODEL
SWIGLU_LIMIT = 7.0
SWIGLU_ALPHA = 1.702
EPS = 1e-5

LANES = 128
BF16_SUBLANES = 16
VMEM_LIMIT = 56 * 1024 * 1024

IN_TM = 512
IN_TN = 1024
ATTN_QB = 256
ATTN_UNROLL_ROWS = 1024
ATTN_FIN = 256
MIX_TM = 256
DISP_R = 256
MOE_SUB = 256
MOE_TILE = 1024
MOE_NSUB = MOE_TILE // MOE_SUB
MOE_TF = 256
FIN_TM = 256


def _params(sem):
    return pltpu.CompilerParams(dimension_semantics=sem, vmem_limit_bytes=VMEM_LIMIT)


def _inproj_body(x_ref, g_ref, w_ref, cos_ref, sa_ref, sb_ref, qkv_ref, gate_ref, h_scr):
    j = pl.program_id(1)

    @pl.when(j == 0)
    def _():
        x = x_ref[...]
        ms = jnp.mean(x * x, axis=-1, keepdims=True)
        h_scr[...] = (x * lax.rsqrt(ms + EPS) * g_ref[...]).astype(BF16)

    acc = jnp.dot(h_scr[...], w_ref[...], preferred_element_type=F32)

    @pl.when(j < 2)
    def _():
        scale = jnp.where(j == 0, HEAD_DIM ** -0.5, 1.0).astype(F32)
        c, sa, sb = cos_ref[...], sa_ref[...], sb_ref[...]
        for h in range(N_HEADS):
            sl = slice(h * HEAD_DIM, (h + 1) * HEAD_DIM)
            a = acc[:, sl]
            r = (a * c + pltpu.roll(a, HEAD_DIM - ROPE_HALF, 1) * sa
                 + pltpu.roll(a, ROPE_HALF, 1) * sb)
            qkv_ref[:, sl] = r * scale

    @pl.when(j == 2)
    def _():
        qkv_ref[...] = acc

    @pl.when(j > 2)
    def _():
        gate_ref[...] = acc.astype(BF16)


def _rope_tables(S):
    inv_freq = ROPE_THETA ** (-jnp.arange(ROPE_HALF, dtype=F32) * 2.0 / ROPE_DIM)
    ang = jnp.arange(S, dtype=F32)[:, None] * inv_freq[None, :]
    cos, sin = jnp.cos(ang), jnp.sin(ang)
    pad = jnp.zeros((S, HEAD_DIM - ROPE_DIM), F32)
    zero = jnp.zeros((S, ROPE_HALF), F32)
    c = jnp.concatenate([cos, cos, pad + 1.0], axis=1)
    sa = jnp.concatenate([-sin, zero, pad], axis=1)
    sb = jnp.concatenate([zero, sin, pad], axis=1)
    return c, sa, sb


def _in_projection(x2d, S, gain, w_bf16, tables):
    T = x2d.shape[0]
    tm = IN_TM
    pos_blocks = S // tm
    n_qkv = QKV_WIDTH // IN_TN
    tab_spec = pl.BlockSpec((tm, HEAD_DIM), lambda i, j: (i % pos_blocks, 0))
    return pl.pallas_call(
        _inproj_body,
        grid=(T // tm, (QKV_WIDTH + GATE_WIDTH) // IN_TN),
        in_specs=[
            pl.BlockSpec((tm, D_MODEL), lambda i, j: (i, 0)),
            pl.BlockSpec((1, D_MODEL), lambda i, j: (0, 0)),
            pl.BlockSpec((D_MODEL, IN_TN), lambda i, j: (0, j)),
            tab_spec, tab_spec, tab_spec,
        ],
        out_specs=[
            pl.BlockSpec((tm, IN_TN), lambda i, j: (i, jnp.minimum(j, n_qkv - 1))),
            pl.BlockSpec((tm, IN_TN), lambda i, j: (i, jnp.maximum(j - n_qkv, 0))),
        ],
        out_shape=[jax.ShapeDtypeStruct((T, QKV_WIDTH), F32), jax.ShapeDtypeStruct((T, GATE_WIDTH), BF16)],
        scratch_shapes=[pltpu.VMEM((tm, D_MODEL), BF16)],
        compiler_params=_params(("arbitrary", "arbitrary")),
        name="in_projection",
    )(x2d, gain, w_bf16, *tables)


def _attn_body(q_ref, k_ref, v_ref, gain_ref, o_ref, qc, kc, vc, og, lg, *, S):
    for g, (window, d) in enumerate(DILATED_PATTERNS):
        L = S // d
        qb = min(ATTN_QB, L)
        half = window // (2 * d)
        W = min(qb + 2 * half, L)
        n_blocks = L // qb
        rel = (lax.broadcasted_iota(jnp.int32, (qb, W), 1) - lax.broadcasted_iota(jnp.int32, (qb, W), 0))

        def rows_of_class(r, start, size, d=d):
            if d == 1:
                return pl.ds(start, size)
            return pl.ds(r + start * d, size, stride=d)

        def stage(r, carry, L=L, qb=qb, rows_of_class=rows_of_class):
            dst = pl.ds(pl.multiple_of(r * L, qb), L)
            qc[dst, :] = q_ref[rows_of_class(r, 0, L), :].astype(BF16)
            kc[dst, :] = k_ref[rows_of_class(r, 0, L), :].astype(BF16)
            vc[dst, :] = v_ref[rows_of_class(r, 0, L), :].astype(BF16)
            return carry

        if d == 1:
            stage(0, 0)
        else:
            lax.fori_loop(0, d, stage, 0)

        def block(nb, c, g=g, L=L, qb=qb, half=half, W=W, n_blocks=n_blocks, rel=rel,
                  rows_of_class=rows_of_class):
            r = nb // n_blocks
            q0 = pl.multiple_of((nb % n_blocks) * qb, qb)
            base = r * L
            ws = jnp.clip(q0 - half, 0, L - W)
            u = rel + (ws - q0 + half)
            valid = (u >= 0) & (u <= 2 * half)
            q = qc[pl.ds(pl.multiple_of(base + q0, qb), qb), :]
            keys = pl.ds(pl.multiple_of(base + ws, BF16_SUBLANES), W)
            s = lax.dot_general(q, kc[keys, :], (((1,), (1,)), ((), ())), preferred_element_type=F32)
            s = jnp.where(valid, s, -1e30)
            m = jnp.max(s, axis=-1, keepdims=True)
            p = jnp.exp(s - m)
            l = jnp.sum(p, axis=-1, keepdims=True)
            acc = jnp.dot(p.astype(BF16), vc[keys, :], preferred_element_type=F32)
            og[g, rows_of_class(r, q0, qb), :] = acc / l
            lg[g, rows_of_class(r, q0, qb), :] = jnp.broadcast_to(m + jnp.log(l), (qb, LANES))
            return c

        lax.fori_loop(0, S // qb, block, 0, unroll=ATTN_UNROLL_ROWS // qb)

    def combine(c, carry):
        rows = pl.ds(pl.multiple_of(c * ATTN_FIN, ATTN_FIN), ATTN_FIN)
        l0, l1, l2 = lg[0, rows, :], lg[1, rows, :], lg[2, rows, :]
        mx = jnp.maximum(jnp.maximum(l0, l1), l2)
        e0, e1, e2 = jnp.exp(l0 - mx), jnp.exp(l1 - mx), jnp.exp(l2 - mx)
        a = (e0 * og[0, rows, :] + e1 * og[1, rows, :] + e2 * og[2, rows, :]) / (e0 + e1 + e2)
        ms = jnp.mean(a * a, axis=-1, keepdims=True)
        o_ref[rows, :] = (a * lax.rsqrt(ms + EPS) * gain_ref[...]).astype(BF16)
        return carry

    lax.fori_loop(0, S // ATTN_FIN, combine, 0)


def _band_attention(qkv, B, S, gain):
    for window, d in DILATED_PATTERNS:
        assert S % (d * min(ATTN_QB, S // d)) == 0 and window % (2 * d) == 0
    head = lambda t: pl.BlockSpec((S, HEAD_DIM), lambda b, h: (b, t * N_HEADS + h))
    return pl.pallas_call(
        functools.partial(_attn_body, S=S),
        grid=(B, N_HEADS),
        in_specs=[head(0), head(1), head(2), pl.BlockSpec((1, HEAD_DIM), lambda b, h: (0, h))],
        out_specs=pl.BlockSpec((S, HEAD_DIM), lambda b, h: (b, h)),
        out_shape=jax.ShapeDtypeStruct((B * S, ATTN_WIDTH), BF16),
        scratch_shapes=[pltpu.VMEM((S, HEAD_DIM), BF16)] * 3
        + [pltpu.VMEM((len(DILATED_PATTERNS), S, HEAD_DIM), F32)] * 2,
        compiler_params=_params(("arbitrary", "arbitrary")),
        name="band_attention",
    )(qkv, qkv, qkv, gain)


def _mix_body(attn_ref, cb_ref, cc_ref, cx_ref, ccp_ref, cxp_ref, ccn_ref, cxn_ref, x_ref, wout_ref,
              convw_ref, gc_ref, g2_ref, wrh_ref, wrl_ref, br_ref,
              x1_ref, h2_ref, idx_ref, gate_ref, conv_scr, *, tm, S):
    i = pl.program_id(0)

    row = lax.broadcasted_iota(jnp.int32, (tm, LANES), 0)
    r0 = i * tm
    at_start = (r0 % S) == 0
    at_end = ((r0 + tm) % S) == 0
    last = BF16_SUBLANES - 1
    gw = CONV_WIDTH // N_CONV_GROUPS
    for c in range(N_CONV_GROUPS):
        sl = slice(c * gw, (c + 1) * gw)
        u = cc_ref[:, sl].astype(F32) * cx_ref[:, sl].astype(F32)
        up = ccp_ref[last:last + 1, sl].astype(F32) * cxp_ref[last:last + 1, sl].astype(F32)
        un = ccn_ref[0:1, sl].astype(F32) * cxn_ref[0:1, sl].astype(F32)
        up = jnp.where(at_start, 0.0, up)
        un = jnp.where(at_end, 0.0, un)
        u_prev = jnp.where(row == 0, up, pltpu.roll(u, 1, 0))
        u_next = jnp.where(row == tm - 1, un, pltpu.roll(u, tm - 1, 0))
        y = convw_ref[0:1, sl] * u_prev + convw_ref[1:2, sl] * u + convw_ref[2:3, sl] * u_next
        cv = cb_ref[:, sl].astype(F32) * y
        ms = jnp.mean(cv * cv, axis=-1, keepdims=True)
        conv_scr[:, sl] = (cv * lax.rsqrt(ms + EPS) * gc_ref[:, sl]).astype(BF16)

    mix = (jnp.dot(attn_ref[...], wout_ref[0:ATTN_WIDTH, :], preferred_element_type=F32)
           + jnp.dot(conv_scr[...], wout_ref[ATTN_WIDTH:D_MODEL, :], preferred_element_type=F32))
    x1 = x_ref[...] + mix
    x1_ref[...] = x1
    ms = jnp.mean(x1 * x1, axis=-1, keepdims=True)
    h2 = x1 * lax.rsqrt(ms + EPS) * g2_ref[...]
    hi = h2.astype(BF16)
    hi_f = hi.astype(F32)
    h2_ref[...] = hi_f

    lo = (h2 - hi_f).astype(BF16)
    logits = (jnp.dot(hi, wrh_ref[...], preferred_element_type=F32)
              + jnp.dot(lo, wrh_ref[...], preferred_element_type=F32)
              + jnp.dot(hi, wrl_ref[...], preferred_element_type=F32)) + br_ref[...]

    lane = lax.broadcasted_iota(jnp.int32, (tm, LANES), 1)
    lane_f = lane.astype(F32)
    vals, ids = [], []
    for _ in range(TOP_K):
        m = jnp.max(logits, axis=-1, keepdims=True)
        ix = jnp.min(jnp.where(logits == m, lane_f, float(LANES)), axis=-1, keepdims=True)
        vals.append(m)
        ids.append(ix)
        logits = jnp.where(lane_f == ix, -jnp.inf, logits)
    es = [jnp.exp(v - vals[0]) for v in vals]
    den = es[0] + es[1] + es[2] + es[3]
    idx_tile = jnp.zeros((tm, LANES), F32)
    gate_tile = jnp.zeros((tm, LANES), F32)
    for k in range(TOP_K):
        idx_tile = jnp.where(lane == k, ids[k], idx_tile)
        gate_tile = jnp.where(lane == k, es[k] / den, gate_tile)
    idx_ref[...] = idx_tile.astype(jnp.int32)
    gate_ref[...] = gate_tile


def _mixer_out(x2d, S, attn, gates3, conv_w, gc, w_out, g2, wr_hi, wr_lo, br):
    T = x2d.shape[0]
    tm = MIX_TM
    hr = BF16_SUBLANES
    n_halo = T // hr
    row_tile = lambda w: pl.BlockSpec((tm, w), lambda i: (i, 0))
    gate_tile = lambda c: pl.BlockSpec((tm, CONV_WIDTH), lambda i: (i, c))
    prev_halo = lambda c: pl.BlockSpec((hr, CONV_WIDTH), lambda i: (jnp.maximum(i * (tm // hr) - 1, 0), c))
    next_halo = lambda c: pl.BlockSpec(
        (hr, CONV_WIDTH), lambda i: (jnp.minimum((i + 1) * (tm // hr), n_halo - 1), c))
    const = lambda shape: pl.BlockSpec(shape, lambda i: (0, 0))
    cb_col, cc_col, cx_col = 0, 1, 2
    return pl.pallas_call(
        functools.partial(_mix_body, tm=tm, S=S),
        grid=(T // tm,),
        in_specs=[
            row_tile(ATTN_WIDTH),
            gate_tile(cb_col), gate_tile(cc_col), gate_tile(cx_col),
            prev_halo(cc_col), prev_halo(cx_col), next_halo(cc_col), next_halo(cx_col),
            row_tile(D_MODEL),
            const((D_MODEL, D_MODEL)),
            const((3, CONV_WIDTH)), const((1, CONV_WIDTH)), const((1, D_MODEL)),
            const((D_MODEL, LANES)), const((D_MODEL, LANES)), const((1, LANES)),
        ],
        out_specs=[row_tile(D_MODEL), row_tile(D_MODEL), row_tile(LANES), row_tile(LANES)],
        out_shape=[
            jax.ShapeDtypeStruct((T, D_MODEL), F32),
            jax.ShapeDtypeStruct((T, D_MODEL), F32),
            jax.ShapeDtypeStruct((T, LANES), jnp.int32),
            jax.ShapeDtypeStruct((T, LANES), F32),
        ],
        scratch_shapes=[pltpu.VMEM((tm, CONV_WIDTH), BF16)],
        compiler_params=_params(("arbitrary",)),
        name="mixer_out_router",
    )(attn, gates3, gates3, gates3, gates3, gates3, gates3, gates3, x2d, w_out, conv_w,
      gc, g2, wr_hi, wr_lo, br)


def _dispatch_body(nf_ref, slot_ref, ha_ref, hb_ref, xs_hbm, zero_scr, zsem, sem, *, steps_a):
    i = pl.program_id(0)
    n_sub = xs_hbm.shape[0] // MOE_SUB

    def zero_copy(sb):
        return pltpu.make_async_copy(zero_scr, xs_hbm.at[pl.ds(sb * MOE_SUB, MOE_SUB), :], zsem)

    @pl.when(i == 0)
    def _():
        zero_scr[...] = jnp.zeros_like(zero_scr)

        def start(sb, cnt):
            pl.when(nf_ref[sb] != 0)(lambda: zero_copy(sb).start())
            return cnt + nf_ref[sb]

        cnt = lax.fori_loop(0, n_sub, start, 0)

        def wait(_, c):
            zero_copy(0).wait()
            return c

        lax.fori_loop(0, cnt, wait, 0)

    def scatter(h_ref):
        def issue(j, c):
            for k in range(TOP_K):
                s = slot_ref[0, j * TOP_K + k]
                pltpu.make_async_copy(h_ref.at[pl.ds(j, 1), :], xs_hbm.at[pl.ds(s, 1), :], sem).start()
            return c
        lax.fori_loop(0, DISP_R, issue, 0, unroll=4)

    pl.when(i < steps_a)(lambda: scatter(ha_ref))
    pl.when(i >= steps_a)(lambda: scatter(hb_ref))
    n_rows = DISP_R * TOP_K
    pltpu.make_async_copy(xs_hbm.at[pl.ds(0, n_rows), :], xs_hbm.at[pl.ds(0, n_rows), :], sem).wait()


def _dispatch(h2_a, h2_b, slot, notfull, P):
    steps_a = h2_a.shape[0] // DISP_R
    steps_b = h2_b.shape[0] // DISP_R
    n_steps = steps_a + steps_b
    grid_spec = pltpu.PrefetchScalarGridSpec(
        num_scalar_prefetch=1,
        grid=(n_steps,),
        in_specs=[
            pl.BlockSpec((None, 1, DISP_R * TOP_K), lambda i, nf: (i, 0, 0), memory_space=pltpu.SMEM),
            pl.BlockSpec((DISP_R, D_MODEL), lambda i, nf: (jnp.minimum(i, steps_a - 1), 0)),
            pl.BlockSpec((DISP_R, D_MODEL), lambda i, nf: (jnp.maximum(i - steps_a, 0), 0)),
        ],
        out_specs=pl.BlockSpec(memory_space=pl.ANY),
        scratch_shapes=[pltpu.VMEM((MOE_SUB, D_MODEL), F32), pltpu.SemaphoreType.DMA(()),
                        pltpu.SemaphoreType.DMA(())],
    )
    return pl.pallas_call(
        functools.partial(_dispatch_body, steps_a=steps_a),
        grid_spec=grid_spec,
        out_shape=jax.ShapeDtypeStruct((P, D_MODEL), F32),
        compiler_params=_params(("arbitrary",)),
        name="dispatch",
    )(notfull, slot.reshape(n_steps, 1, DISP_R * TOP_K), h2_a, h2_b)


def _expert_body(te_ref, tx_ref, ns_ref, x_ref, wg_ref, bg_ref, wu_ref, bu_ref, wd_ref, bd_ref, y_ref):
    i = pl.program_id(0)
    f = pl.program_id(1)
    nsub = ns_ref[i]

    @pl.when(f == 0)
    def _():
        for s in range(MOE_NSUB):
            rows = slice(s * MOE_SUB, (s + 1) * MOE_SUB)
            bias = jnp.broadcast_to(bd_ref[...], (MOE_SUB, D_MODEL))
            y_ref[rows, :] = jnp.where(s < nsub, bias, 0.0)

    def sub_block(s, wg, wu, wd):
        rows = slice(s * MOE_SUB, (s + 1) * MOE_SUB)
        x = x_ref[rows, :].astype(BF16)
        g = jnp.dot(x, wg, preferred_element_type=F32) + bg_ref[...]
        u = jnp.dot(x, wu, preferred_element_type=F32) + bu_ref[...]
        g = jnp.minimum(g, SWIGLU_LIMIT)
        u = jnp.clip(u, -SWIGLU_LIMIT, SWIGLU_LIMIT)
        a = (u + 1.0) * (g * jax.nn.sigmoid(SWIGLU_ALPHA * g))
        y_ref[rows, :] += jnp.dot(a.astype(BF16), wd, preferred_element_type=F32)

    @pl.when(nsub == MOE_NSUB)
    def _():
        wg, wu, wd = wg_ref[...].astype(BF16), wu_ref[...].astype(BF16), wd_ref[...].astype(BF16)
        for s in range(MOE_NSUB):
            sub_block(s, wg, wu, wd)

    @pl.when((nsub > 0) & (nsub < MOE_NSUB))
    def _():
        wg, wu, wd = wg_ref[...].astype(BF16), wu_ref[...].astype(BF16), wd_ref[...].astype(BF16)
        for s in range(MOE_NSUB - 1):
            pl.when(s < nsub)(functools.partial(sub_block, s, wg, wu, wd))


def _expert_ffn(xs, tile_expert, tile_xblk, tile_nsub, w_gate, b_gate, w_up, b_up, w_down, b_down):
    P = xs.shape[0]
    n_tiles = P // MOE_TILE
    nf = D_FF // MOE_TF
    fsel = lambda i, f, ns: jnp.where(ns[i] > 0, f, nf - 1)
    grid_spec = pltpu.PrefetchScalarGridSpec(
        num_scalar_prefetch=3,
        grid=(n_tiles, nf),
        in_specs=[
            pl.BlockSpec((MOE_TILE, D_MODEL), lambda i, f, te, tx, ns: (tx[i], 0)),
            pl.BlockSpec((None, D_MODEL, MOE_TF), lambda i, f, te, tx, ns: (te[i], 0, fsel(i, f, ns))),
            pl.BlockSpec((None, 1, MOE_TF), lambda i, f, te, tx, ns: (te[i], 0, fsel(i, f, ns))),
            pl.BlockSpec((None, D_MODEL, MOE_TF), lambda i, f, te, tx, ns: (te[i], 0, fsel(i, f, ns))),
            pl.BlockSpec((None, 1, MOE_TF), lambda i, f, te, tx, ns: (te[i], 0, fsel(i, f, ns))),
            pl.BlockSpec((None, MOE_TF, D_MODEL), lambda i, f, te, tx, ns: (te[i], fsel(i, f, ns), 0)),
            pl.BlockSpec((None, 1, D_MODEL), lambda i, f, te, tx, ns: (te[i], 0, 0)),
        ],
        out_specs=pl.BlockSpec((MOE_TILE, D_MODEL), lambda i, f, te, tx, ns: (i, 0)),
    )
    return pl.pallas_call(
        _expert_body,
        grid_spec=grid_spec,
        out_shape=jax.ShapeDtypeStruct((P, D_MODEL), F32),
        compiler_params=_params(("arbitrary", "arbitrary")),
        name="expert_ffn",
    )(tile_expert, tile_xblk, tile_nsub, xs, w_gate, b_gate.reshape(N_EXPERTS, 1, D_FF),
      w_up, b_up.reshape(N_EXPERTS, 1, D_FF), w_down, b_down.reshape(N_EXPERTS, 1, D_MODEL))


def _final_body(slot_ref, slot_next_ref, x1_ref, gate_ref, gf_ref, y_hbm, o_ref, ybuf, sems, *, n):
    i = pl.program_id(0)
    tm = FIN_TM

    def gather(s_ref, b):
        def body(j, c):
            for k in range(TOP_K):
                s = s_ref[0, j * TOP_K + k]
                pltpu.make_async_copy(y_hbm.at[pl.ds(s, 1), :], ybuf.at[b, k, pl.ds(j, 1), :], sems.at[b]).start()
            return c
        lax.fori_loop(0, tm, body, 0, unroll=4)

    b = i % 2
    pl.when(i == 0)(lambda: gather(slot_ref, 0))
    pl.when(i + 1 < n)(lambda: gather(slot_next_ref, 1 - b))
    pltpu.make_async_copy(ybuf.at[b], ybuf.at[b], sems.at[b]).wait()

    gates = gate_ref[...]
    acc = x1_ref[...]
    for k in range(TOP_K):
        acc = acc + gates[:, k:k + 1] * ybuf[b, k]
    ms = jnp.mean(acc * acc, axis=-1, keepdims=True)
    o_ref[...] = acc * lax.rsqrt(ms + EPS) * gf_ref[...]


def _combine_final(x1, gates, slot, y, gf):
    T = x1.shape[0]
    tm = FIN_TM
    n_steps = T // tm
    row_tile = lambda w: pl.BlockSpec((tm, w), lambda i: (i, 0))
    slot3 = slot.reshape(n_steps, 1, tm * TOP_K)
    slot_spec = lambda step: pl.BlockSpec((None, 1, tm * TOP_K), step, memory_space=pltpu.SMEM)
    return pl.pallas_call(
        functools.partial(_final_body, n=n_steps),
        grid=(n_steps,),
        in_specs=[
            slot_spec(lambda i: (i, 0, 0)),
            slot_spec(lambda i: (jnp.minimum(i + 1, n_steps - 1), 0, 0)),
            row_tile(D_MODEL), row_tile(LANES),
            pl.BlockSpec((1, D_MODEL), lambda i: (0, 0)),
            pl.BlockSpec(memory_space=pl.ANY),
        ],
        out_specs=row_tile(D_MODEL),
        out_shape=jax.ShapeDtypeStruct((T, D_MODEL), F32),
        scratch_shapes=[pltpu.VMEM((2, TOP_K, tm, D_MODEL), F32), pltpu.SemaphoreType.DMA((2,))],
        compiler_params=_params(("arbitrary",)),
        name="combine_final_norm",
    )(slot3, slot3, x1, gates, gf, y)


def _route(idx):
    n = idx.shape[0] * TOP_K
    n_tiles = n // MOE_TILE + N_EXPERTS
    e_flat = idx.reshape(-1)
    onehot = (e_flat[:, None] == jnp.arange(N_EXPERTS, dtype=jnp.int32)[None, :]).astype(jnp.int32)
    csum = jnp.cumsum(onehot, axis=0)
    rank = jnp.take_along_axis(csum, e_flat[:, None], axis=1)[:, 0] - 1
    counts = csum[-1]
    tiles_per_e = (counts + MOE_TILE - 1) // MOE_TILE
    tile_end = jnp.cumsum(tiles_per_e)
    tile_start = tile_end - tiles_per_e
    slot = tile_start[e_flat] * MOE_TILE + rank

    t = jnp.arange(n_tiles, dtype=jnp.int32)
    used = t < tile_end[-1]
    last_used = jnp.maximum(tile_end[-1] - 1, 0)
    tt = jnp.where(used, t, last_used)
    te = jnp.minimum(jnp.sum(tt[:, None] >= tile_end[None, :], axis=1), N_EXPERTS - 1).astype(jnp.int32)
    rows = jnp.where(used, jnp.clip(counts[te] - (tt - tile_start[te]) * MOE_TILE, 0, MOE_TILE), 0)
    nsub = ((rows + MOE_SUB - 1) // MOE_SUB).astype(jnp.int32)
    sub_end = (jnp.arange(MOE_NSUB, dtype=jnp.int32) + 1) * MOE_SUB
    notfull = (rows[:, None] < sub_end[None, :]).astype(jnp.int32).reshape(-1)
    return slot.reshape(-1, TOP_K), notfull, te, tt.astype(jnp.int32), nsub, n_tiles * MOE_TILE


def kernel(x_prompt, x_sample, norm1_g, w_in, conv_w, attn_norm_g, conv_norm_g, w_out, norm2_g,
           w_router, b_router, w_gate, b_gate, w_up, b_up, w_down, b_down, final_norm_g):
    assert norm1_g.shape[0] == 1, "one layer"
    w_in_b = w_in[0].astype(BF16)
    w_out_b = w_out[0].astype(BF16)
    wr = jnp.pad(w_router[0], ((0, 0), (0, LANES - N_EXPERTS)))
    wr_hi = wr.astype(BF16)
    wr_lo = (wr - wr_hi.astype(F32)).astype(BF16)
    br = jnp.pad(b_router[0], (0, LANES - N_EXPERTS), constant_values=-jnp.inf).reshape(1, LANES)
    g1 = norm1_g[0].reshape(1, D_MODEL)
    g2 = norm2_g[0].reshape(1, D_MODEL)
    ga = attn_norm_g[0].reshape(1, ATTN_WIDTH)
    gc = conv_norm_g[0].reshape(1, CONV_WIDTH)
    gf = final_norm_g.reshape(1, D_MODEL)

    x1s, h2s, idxs, gates = [], [], [], []
    for x in (x_prompt, x_sample):
        B, S, _ = x.shape
        x2d = x.reshape(B * S, D_MODEL)
        qkv, gates3 = _in_projection(x2d, S, g1, w_in_b, _rope_tables(S))
        attn = _band_attention(qkv, B, S, ga)
        x1, h2, idx, gate = _mixer_out(x2d, S, attn, gates3, conv_w[0], gc, w_out_b, g2, wr_hi, wr_lo, br)
        x1s.append(x1)
        h2s.append(h2)
        idxs.append(idx[:, :TOP_K])
        gates.append(gate)

    slot, notfull, te, tx, nsub, P = _route(jnp.concatenate(idxs, axis=0))
    slots = []
    t0 = 0
    for h2 in h2s:
        slots.append(slot[t0:t0 + h2.shape[0]])
        t0 += h2.shape[0]
    xs = _dispatch(h2s[0], h2s[1], slot, notfull, P)
    y = _expert_ffn(xs, te, tx, nsub, w_gate[0], b_gate[0], w_up[0], b_up[0], w_down[0], b_down[0])
    return tuple(_combine_final(x1, gate, sl, y, gf).reshape(x.shape)
                 for x, x1, gate, sl in zip((x_prompt, x_sample), x1s, gates, slots))
```

```python
import functools

import jax
import jax.numpy as jnp
from jax import lax
from jax.experimental import pallas as pl
from jax.experimental.pallas import tpu as pltpu

F32 = jnp.float32
BF16 = jnp.bfloat16

D_MODEL = 2048
HEAD_DIM = 128
N_HEADS = 8
ATTN_WIDTH = N_HEADS * HEAD_DIM
CONV_WIDTH = D_MODEL - ATTN_WIDTH
N_CONV_GROUPS = 8
QKV_WIDTH = 3 * ATTN_WIDTH
GATE_WIDTH = 3 * CONV_WIDTH
DILATED_PATTERNS = ((128, 1), (512, 4), (2048, 16))
ROPE_THETA = 500000.0
ROPE_DIM = HEAD_DIM // 4
ROPE_HALF = ROPE_DIM // 2
N_EXPERTS = 32
TOP_K = 4
D_FF = D_MODEL
SWIGLU_LIMIT = 7.0
SWIGLU_ALPHA = 1.702
EPS = 1e-5

LANES = 128
BF16_SUBLANES = 16
VMEM_LIMIT = 56 * 1024 * 1024

IN_TM = 512
IN_TN = 1024
ATTN_QB = 128
ATTN_UNROLL = 8
ATTN_FIN = 256
MIX_TM = 256
DISP_R = 256
MOE_SUB = 256
MOE_TILE = 1024
MOE_NSUB = MOE_TILE // MOE_SUB
MOE_TF = 256
FIN_TM = 256


def _params(sem):
    return pltpu.CompilerParams(dimension_semantics=sem, vmem_limit_bytes=VMEM_LIMIT)


def _inproj_body(x_ref, g_ref, w_ref, cos_ref, sa_ref, sb_ref, qkv_ref, gate_ref, h_scr):
    j = pl.program_id(1)

    @pl.when(j == 0)
    def _():
        x = x_ref[...]
        ms = jnp.mean(x * x, axis=-1, keepdims=True)
        h_scr[...] = (x * lax.rsqrt(ms + EPS) * g_ref[...]).astype(BF16)

    acc = jnp.dot(h_scr[...], w_ref[...], preferred_element_type=F32)

    @pl.when(j < 2)
    def _():
        scale = jnp.where(j == 0, HEAD_DIM ** -0.5, 1.0).astype(F32)
        c, sa, sb = cos_ref[...], sa_ref[...], sb_ref[...]
        for h in range(N_HEADS):
            sl = slice(h * HEAD_DIM, (h + 1) * HEAD_DIM)
            a = acc[:, sl]
            r = (a * c + pltpu.roll(a, HEAD_DIM - ROPE_HALF, 1) * sa
                 + pltpu.roll(a, ROPE_HALF, 1) * sb)
            qkv_ref[:, sl] = r * scale

    @pl.when(j == 2)
    def _():
        qkv_ref[...] = acc

    @pl.when(j > 2)
    def _():
        gate_ref[...] = acc.astype(BF16)


def _rope_tables(S):
    inv_freq = ROPE_THETA ** (-jnp.arange(ROPE_HALF, dtype=F32) * 2.0 / ROPE_DIM)
    ang = jnp.arange(S, dtype=F32)[:, None] * inv_freq[None, :]
    cos, sin = jnp.cos(ang), jnp.sin(ang)
    pad = jnp.zeros((S, HEAD_DIM - ROPE_DIM), F32)
    zero = jnp.zeros((S, ROPE_HALF), F32)
    c = jnp.concatenate([cos, cos, pad + 1.0], axis=1)
    sa = jnp.concatenate([-sin, zero, pad], axis=1)
    sb = jnp.concatenate([zero, sin, pad], axis=1)
    return c, sa, sb


def _in_projection(x2d, S, gain, w_bf16, tables):
    T = x2d.shape[0]
    tm = IN_TM
    pos_blocks = S // tm
    n_qkv = QKV_WIDTH // IN_TN
    tab_spec = pl.BlockSpec((tm, HEAD_DIM), lambda i, j: (i % pos_blocks, 0))
    return pl.pallas_call(
        _inproj_body,
        grid=(T // tm, (QKV_WIDTH + GATE_WIDTH) // IN_TN),
        in_specs=[
            pl.BlockSpec((tm, D_MODEL), lambda i, j: (i, 0)),
            pl.BlockSpec((1, D_MODEL), lambda i, j: (0, 0)),
            pl.BlockSpec((D_MODEL, IN_TN), lambda i, j: (0, j)),
            tab_spec, tab_spec, tab_spec,
        ],
        out_specs=[
            pl.BlockSpec((tm, IN_TN), lambda i, j: (i, jnp.minimum(j, n_qkv - 1))),
            pl.BlockSpec((tm, IN_TN), lambda i, j: (i, jnp.maximum(j - n_qkv, 0))),
        ],
        out_shape=[jax.ShapeDtypeStruct((T, QKV_WIDTH), F32), jax.ShapeDtypeStruct((T, GATE_WIDTH), BF16)],
        scratch_shapes=[pltpu.VMEM((tm, D_MODEL), BF16)],
        compiler_params=_params(("arbitrary", "arbitrary")),
        name="in_projection",
    )(x2d, gain, w_bf16, *tables)


def _attn_configs(S):
    cfgs, tables = [], []
    for window, d in DILATED_PATTERNS:
        L = S // d
        qb = min(ATTN_QB, L)
        half = window // (2 * d)
        W = min(qb + 2 * half, L)
        assert L % qb == 0 and window % (2 * d) == 0
        assert (W == qb + 2 * half and L // qb >= 2) or (L == qb == W)
        if (qb, W, half) not in tables:
            tables.append((qb, W, half))
        cfgs.append((d, L, qb, W, half, tables.index((qb, W, half))))
    return cfgs, tables


def _mask_table(qb, W, half):
    rel = jnp.arange(W, dtype=jnp.int32)[None, :] - jnp.arange(qb, dtype=jnp.int32)[:, None]
    offs = jnp.array([half, 0, -half], jnp.int32)[:, None, None]
    u = rel[None] + offs
    return jnp.where((u >= 0) & (u <= 2 * half), 0.0, -1e30).astype(F32)


def _attn_body(*refs, S):
    cfgs, tables = _attn_configs(S)
    q_ref, k_ref, v_ref, gain_ref = refs[:4]
    mask_refs = refs[4:4 + len(tables)]
    o_ref, qc, kc, vc, og, lg, sc, mc, q32, k32, v32 = refs[4 + len(tables):]
    srcs, stages, f32s = (q_ref, k_ref, v_ref), (qc, kc, vc), (q32, k32, v32)
    d_prev = 1
    for g, (d, L, qb, W, half, tab) in enumerate(cfgs):
        n_blocks = L // qb
        mask_ref = mask_refs[tab]

        def rows_of_class(r, start, size, d=d):
            if d == 1:
                return pl.ds(start, size)
            return pl.ds(r + start * d, size, stride=d)

        from_prev = d_prev > 1 and d % d_prev == 0
        keep_f32 = any(c[0] > d and c[0] % d == 0 for c in cfgs) and d > 1
        step = d // d_prev if from_prev else d

        def stage(r, carry, d=d, L=L, qb=qb, from_prev=from_prev, keep_f32=keep_f32, step=step, d_prev=d_prev):
            dst = pl.ds(pl.multiple_of(r * L, qb), L)
            for src, st, f32 in zip(srcs, stages, f32s):
                if from_prev:
                    start = (r % d_prev) * (L * step) + r // d_prev
                    x = f32[pl.ds(start, L, stride=step), :]
                elif d == 1:
                    x = src[pl.ds(0, L), :]
                else:
                    x = src[pl.ds(r, L, stride=d), :]
                if keep_f32:
                    f32[dst, :] = x
                st[dst, :] = x.astype(BF16)
            return carry

        if d == 1:
            stage(0, 0)
        else:
            lax.fori_loop(0, d, stage, 0)
        d_prev = d

        def place(nb, L=L, qb=qb, half=half, W=W, n_blocks=n_blocks):
            r = nb // n_blocks
            q0 = pl.multiple_of((nb % n_blocks) * qb, qb)
            ws = jnp.clip(q0 - half, 0, L - W)
            case = jnp.where(q0 == 0, 0, jnp.where(q0 == L - qb, 2, 1))
            keys = pl.ds(pl.multiple_of(r * L + ws, BF16_SUBLANES), W)
            return r, q0, keys, case

        def scores(nb, c, g=g, L=L, qb=qb, W=W, mask_ref=mask_ref, place=place, rows_of_class=rows_of_class):
            r, q0, keys, case = place(nb)
            q = qc[pl.ds(pl.multiple_of(r * L + q0, qb), qb), :]
            s = lax.dot_general(q, kc[keys, :], (((1,), (1,)), ((), ())), preferred_element_type=F32)
            s = s + mask_ref[case]
            m = jnp.max(s, axis=-1, keepdims=True)
            sc[nb, 0:qb, 0:W] = s - m
            mc[pl.ds(pl.multiple_of(nb * qb, qb), qb), :] = jnp.broadcast_to(m, (qb, LANES))
            return c

        def values(nb, c, g=g, qb=qb, W=W, place=place, rows_of_class=rows_of_class):
            r, q0, keys, _ = place(nb)
            p = jnp.exp(sc[nb, 0:qb, 0:W])
            l = jnp.sum(p, axis=-1, keepdims=True)
            acc = jnp.dot(p.astype(BF16), vc[keys, :], preferred_element_type=F32)
            og[g, rows_of_class(r, q0, qb), :] = acc / l
            lg[g, rows_of_class(r, q0, qb), :] = mc[pl.ds(pl.multiple_of(nb * qb, qb), qb), :] + jnp.log(l)
            return c

        unroll = min(ATTN_UNROLL, S // qb)
        lax.fori_loop(0, S // qb, scores, 0, unroll=unroll)
        lax.fori_loop(0, S // qb, values, 0, unroll=unroll)

    def combine(c, carry):
        rows = pl.ds(pl.multiple_of(c * ATTN_FIN, ATTN_FIN), ATTN_FIN)
        l0, l1, l2 = lg[0, rows, :], lg[1, rows, :], lg[2, rows, :]
        mx = jnp.maximum(jnp.maximum(l0, l1), l2)
        e0, e1, e2 = jnp.exp(l0 - mx), jnp.exp(l1 - mx), jnp.exp(l2 - mx)
        a = (e0 * og[0, rows, :] + e1 * og[1, rows, :] + e2 * og[2, rows, :]) / (e0 + e1 + e2)
        ms = jnp.mean(a * a, axis=-1, keepdims=True)
        o_ref[rows, :] = (a * lax.rsqrt(ms + EPS) * gain_ref[...]).astype(BF16)
        return carry

    lax.fori_loop(0, S // ATTN_FIN, combine, 0)


def _band_attention(qkv, B, S, gain):
    cfgs, tables = _attn_configs(S)
    score_shape = (max(S // c[2] for c in cfgs), max(c[2] for c in cfgs), max(c[3] for c in cfgs))
    head = lambda t: pl.BlockSpec((S, HEAD_DIM), lambda b, h: (b, t * N_HEADS + h))
    masks = [_mask_table(*t) for t in tables]
    mask_specs = [pl.BlockSpec(m.shape, lambda b, h: (0, 0, 0)) for m in masks]
    return pl.pallas_call(
        functools.partial(_attn_body, S=S),
        grid=(B, N_HEADS),
        in_specs=[head(0), head(1), head(2), pl.BlockSpec((1, HEAD_DIM), lambda b, h: (0, h))] + mask_specs,
        out_specs=pl.BlockSpec((S, HEAD_DIM), lambda b, h: (b, h)),
        out_shape=jax.ShapeDtypeStruct((B * S, ATTN_WIDTH), BF16),
        scratch_shapes=[pltpu.VMEM((S, HEAD_DIM), BF16)] * 3
        + [pltpu.VMEM((len(DILATED_PATTERNS), S, HEAD_DIM), F32)] * 2
        + [pltpu.VMEM(score_shape, F32)] + [pltpu.VMEM((S, HEAD_DIM), F32)] * 4,
        compiler_params=_params(("arbitrary", "arbitrary")),
        name="band_attention",
    )(qkv, qkv, qkv, gain, *masks)


def _mix_body(attn_ref, cb_ref, cc_ref, cx_ref, ccp_ref, cxp_ref, ccn_ref, cxn_ref, x_ref, wout_ref,
              convw_ref, gc_ref, g2_ref, wrh_ref, wrl_ref, br_ref,
              x1_ref, h2_ref, idx_ref, gate_ref, conv_scr, *, tm, S):
    i = pl.program_id(0)

    row = lax.broadcasted_iota(jnp.int32, (tm, LANES), 0)
    r0 = i * tm
    at_start = (r0 % S) == 0
    at_end = ((r0 + tm) % S) == 0
    last = BF16_SUBLANES - 1
    gw = CONV_WIDTH // N_CONV_GROUPS
    for c in range(N_CONV_GROUPS):
        sl = slice(c * gw, (c + 1) * gw)
        u = cc_ref[:, sl].astype(F32) * cx_ref[:, sl].astype(F32)
        up = ccp_ref[last:last + 1, sl].astype(F32) * cxp_ref[last:last + 1, sl].astype(F32)
        un = ccn_ref[0:1, sl].astype(F32) * cxn_ref[0:1, sl].astype(F32)
        up = jnp.where(at_start, 0.0, up)
        un = jnp.where(at_end, 0.0, un)
        u_prev = jnp.where(row == 0, up, pltpu.roll(u, 1, 0))
        u_next = jnp.where(row == tm - 1, un, pltpu.roll(u, tm - 1, 0))
        y = convw_ref[0:1, sl] * u_prev + convw_ref[1:2, sl] * u + convw_ref[2:3, sl] * u_next
        cv = cb_ref[:, sl].astype(F32) * y
        ms = jnp.mean(cv * cv, axis=-1, keepdims=True)
        conv_scr[:, sl] = (cv * lax.rsqrt(ms + EPS) * gc_ref[:, sl]).astype(BF16)

    mix = (jnp.dot(attn_ref[...], wout_ref[0:ATTN_WIDTH, :], preferred_element_type=F32)
           + jnp.dot(conv_scr[...], wout_ref[ATTN_WIDTH:D_MODEL, :], preferred_element_type=F32))
    x1 = x_ref[...] + mix
    x1_ref[...] = x1
    ms = jnp.mean(x1 * x1, axis=-1, keepdims=True)
    h2 = x1 * lax.rsqrt(ms + EPS) * g2_ref[...]
    hi = h2.astype(BF16)
    hi_f = hi.astype(F32)
    h2_ref[...] = hi_f

    lo = (h2 - hi_f).astype(BF16)
    logits = (jnp.dot(hi, wrh_ref[...], preferred_element_type=F32)
              + jnp.dot(lo, wrh_ref[...], preferred_element_type=F32)
              + jnp.dot(hi, wrl_ref[...], preferred_element_type=F32)) + br_ref[...]

    lane = lax.broadcasted_iota(jnp.int32, (tm, LANES), 1)
    lane_f = lane.astype(F32)
    vals, ids = [], []
    for _ in range(TOP_K):
        m = jnp.max(logits, axis=-1, keepdims=True)
        ix = jnp.min(jnp.where(logits == m, lane_f, float(LANES)), axis=-1, keepdims=True)
        vals.append(m)
        ids.append(ix)
        logits = jnp.where(lane_f == ix, -jnp.inf, logits)
    es = [jnp.exp(v - vals[0]) for v in vals]
    den = es[0] + es[1] + es[2] + es[3]
    idx_tile = jnp.zeros((tm, LANES), F32)
    gate_tile = jnp.zeros((tm, LANES), F32)
    for k in range(TOP_K):
        idx_tile = jnp.where(lane == k, ids[k], idx_tile)
        gate_tile = jnp.where(lane == k, es[k] / den, gate_tile)
    idx_ref[...] = idx_tile.astype(jnp.int32)
    gate_ref[...] = gate_tile


def _mixer_out(x2d, S, attn, gates3, conv_w, gc, w_out, g2, wr_hi, wr_lo, br):
    T = x2d.shape[0]
    tm = MIX_TM
    hr = BF16_SUBLANES
    n_halo = T // hr
    row_tile = lambda w: pl.BlockSpec((tm, w), lambda i: (i, 0))
    gate_tile = lambda c: pl.BlockSpec((tm, CONV_WIDTH), lambda i: (i, c))
    prev_halo = lambda c: pl.BlockSpec((hr, CONV_WIDTH), lambda i: (jnp.maximum(i * (tm // hr) - 1, 0), c))
    next_halo = lambda c: pl.BlockSpec(
        (hr, CONV_WIDTH), lambda i: (jnp.minimum((i + 1) * (tm // hr), n_halo - 1), c))
    const = lambda shape: pl.BlockSpec(shape, lambda i: (0, 0))
    cb_col, cc_col, cx_col = 0, 1, 2
    return pl.pallas_call(
        functools.partial(_mix_body, tm=tm, S=S),
        grid=(T // tm,),
        in_specs=[
            row_tile(ATTN_WIDTH),
            gate_tile(cb_col), gate_tile(cc_col), gate_tile(cx_col),
            prev_halo(cc_col), prev_halo(cx_col), next_halo(cc_col), next_halo(cx_col),
            row_tile(D_MODEL),
            const((D_MODEL, D_MODEL)),
            const((3, CONV_WIDTH)), const((1, CONV_WIDTH)), const((1, D_MODEL)),
            const((D_MODEL, LANES)), const((D_MODEL, LANES)), const((1, LANES)),
        ],
        out_specs=[row_tile(D_MODEL), row_tile(D_MODEL), row_tile(LANES), row_tile(LANES)],
        out_shape=[
            jax.ShapeDtypeStruct((T, D_MODEL), F32),
            jax.ShapeDtypeStruct((T, D_MODEL), F32),
            jax.ShapeDtypeStruct((T, LANES), jnp.int32),
            jax.ShapeDtypeStruct((T, LANES), F32),
        ],
        scratch_shapes=[pltpu.VMEM((tm, CONV_WIDTH), BF16)],
        compiler_params=_params(("arbitrary",)),
        name="mixer_out_router",
    )(attn, gates3, gates3, gates3, gates3, gates3, gates3, gates3, x2d, w_out, conv_w,
      gc, g2, wr_hi, wr_lo, br)


def _dispatch_body(nf_ref, slot_ref, ha_ref, hb_ref, xs_hbm, zero_scr, zsem, sem, *, steps_a):
    i = pl.program_id(0)
    n_sub = xs_hbm.shape[0] // MOE_SUB

    def zero_copy(sb):
        return pltpu.make_async_copy(zero_scr, xs_hbm.at[pl.ds(sb * MOE_SUB, MOE_SUB), :], zsem)

    @pl.when(i == 0)
    def _():
        zero_scr[...] = jnp.zeros_like(zero_scr)

        def start(sb, cnt):
            pl.when(nf_ref[sb] != 0)(lambda: zero_copy(sb).start())
            return cnt + nf_ref[sb]

        cnt = lax.fori_loop(0, n_sub, start, 0)

        def wait(_, c):
            zero_copy(0).wait()
            return c

        lax.fori_loop(0, cnt, wait, 0)

    def scatter(h_ref):
        def issue(j, c):
            for k in range(TOP_K):
                s = slot_ref[0, j * TOP_K + k]
                pltpu.make_async_copy(h_ref.at[pl.ds(j, 1), :], xs_hbm.at[pl.ds(s, 1), :], sem).start(priority=k % 2)
            return c
        lax.fori_loop(0, DISP_R, issue, 0, unroll=4)

    pl.when(i < steps_a)(lambda: scatter(ha_ref))
    pl.when(i >= steps_a)(lambda: scatter(hb_ref))
    n_rows = DISP_R * TOP_K
    pltpu.make_async_copy(xs_hbm.at[pl.ds(0, n_rows), :], xs_hbm.at[pl.ds(0, n_rows), :], sem).wait()


def _dispatch(h2_a, h2_b, slot, notfull, P):
    steps_a = h2_a.shape[0] // DISP_R
    steps_b = h2_b.shape[0] // DISP_R
    n_steps = steps_a + steps_b
    grid_spec = pltpu.PrefetchScalarGridSpec(
        num_scalar_prefetch=1,
        grid=(n_steps,),
        in_specs=[
            pl.BlockSpec((None, 1, DISP_R * TOP_K), lambda i, nf: (i, 0, 0), memory_space=pltpu.SMEM),
            pl.BlockSpec((DISP_R, D_MODEL), lambda i, nf: (jnp.minimum(i, steps_a - 1), 0)),
            pl.BlockSpec((DISP_R, D_MODEL), lambda i, nf: (jnp.maximum(i - steps_a, 0), 0)),
        ],
        out_specs=pl.BlockSpec(memory_space=pl.ANY),
        scratch_shapes=[pltpu.VMEM((MOE_SUB, D_MODEL), F32), pltpu.SemaphoreType.DMA(()),
                        pltpu.SemaphoreType.DMA(())],
    )
    return pl.pallas_call(
        functools.partial(_dispatch_body, steps_a=steps_a),
        grid_spec=grid_spec,
        out_shape=jax.ShapeDtypeStruct((P, D_MODEL), F32),
        compiler_params=_params(("arbitrary",)),
        name="dispatch",
    )(notfull, slot.reshape(n_steps, 1, DISP_R * TOP_K), h2_a, h2_b)


def _expert_body(te_ref, tx_ref, ns_ref, x_ref, wg_ref, bg_ref, wu_ref, bu_ref, wd_ref, bd_ref, y_ref):
    i = pl.program_id(0)
    f = pl.program_id(1)
    nsub = ns_ref[i]

    @pl.when(f == 0)
    def _():
        for s in range(MOE_NSUB):
            rows = slice(s * MOE_SUB, (s + 1) * MOE_SUB)
            bias = jnp.broadcast_to(bd_ref[...], (MOE_SUB, D_MODEL))
            y_ref[rows, :] = jnp.where(s < nsub, bias, 0.0)

    def sub_block(s, wg, wu, wd):
        rows = slice(s * MOE_SUB, (s + 1) * MOE_SUB)
        x = x_ref[rows, :].astype(BF16)
        g = jnp.dot(x, wg, preferred_element_type=F32) + bg_ref[...]
        u = jnp.dot(x, wu, preferred_element_type=F32) + bu_ref[...]
        g = jnp.minimum(g, SWIGLU_LIMIT)
        u = jnp.clip(u, -SWIGLU_LIMIT, SWIGLU_LIMIT)
        a = (u + 1.0) * (g * jax.nn.sigmoid(SWIGLU_ALPHA * g))
        y_ref[rows, :] += jnp.dot(a.astype(BF16), wd, preferred_element_type=F32)

    @pl.when(nsub == MOE_NSUB)
    def _():
        wg, wu, wd = wg_ref[...].astype(BF16), wu_ref[...].astype(BF16), wd_ref[...].astype(BF16)
        for s in range(MOE_NSUB):
            sub_block(s, wg, wu, wd)

    @pl.when((nsub > 0) & (nsub < MOE_NSUB))
    def _():
        wg, wu, wd = wg_ref[...].astype(BF16), wu_ref[...].astype(BF16), wd_ref[...].astype(BF16)
        for s in range(MOE_NSUB - 1):
            pl.when(s < nsub)(functools.partial(sub_block, s, wg, wu, wd))


def _expert_ffn(xs, tile_expert, tile_xblk, tile_nsub, w_gate, b_gate, w_up, b_up, w_down, b_down):
    P = xs.shape[0]
    n_tiles = P // MOE_TILE
    nf = D_FF // MOE_TF
    fsel = lambda i, f, ns: jnp.where(ns[i] > 0, f, nf - 1)
    grid_spec = pltpu.PrefetchScalarGridSpec(
        num_scalar_prefetch=3,
        grid=(n_tiles, nf),
        in_specs=[
            pl.BlockSpec((MOE_TILE, D_MODEL), lambda i, f, te, tx, ns: (tx[i], 0)),
            pl.BlockSpec((None, D_MODEL, MOE_TF), lambda i, f, te, tx, ns: (te[i], 0, fsel(i, f, ns))),
            pl.BlockSpec((None, 1, MOE_TF), lambda i, f, te, tx, ns: (te[i], 0, fsel(i, f, ns))),
            pl.BlockSpec((None, D_MODEL, MOE_TF), lambda i, f, te, tx, ns: (te[i], 0, fsel(i, f, ns))),
            pl.BlockSpec((None, 1, MOE_TF), lambda i, f, te, tx, ns: (te[i], 0, fsel(i, f, ns))),
            pl.BlockSpec((None, MOE_TF, D_MODEL), lambda i, f, te, tx, ns: (te[i], fsel(i, f, ns), 0)),
            pl.BlockSpec((None, 1, D_MODEL), lambda i, f, te, tx, ns: (te[i], 0, 0)),
        ],
        out_specs=pl.BlockSpec((MOE_TILE, D_MODEL), lambda i, f, te, tx, ns: (i, 0)),
    )
    return pl.pallas_call(
        _expert_body,
        grid_spec=grid_spec,
        out_shape=jax.ShapeDtypeStruct((P, D_MODEL), F32),
        compiler_params=_params(("arbitrary", "arbitrary")),
        name="expert_ffn",
    )(tile_expert, tile_xblk, tile_nsub, xs, w_gate, b_gate.reshape(N_EXPERTS, 1, D_FF),
      w_up, b_up.reshape(N_EXPERTS, 1, D_FF), w_down, b_down.reshape(N_EXPERTS, 1, D_MODEL))


def _final_body(slot_ref, slot_next_ref, x1_ref, gate_ref, gf_ref, y_hbm, o_ref, ybuf, sems, *, n):
    i = pl.program_id(0)
    tm = FIN_TM

    def gather(s_ref, b):
        def body(j, c):
            for k in range(TOP_K):
                s = s_ref[0, j * TOP_K + k]
                pltpu.make_async_copy(y_hbm.at[pl.ds(s, 1), :], ybuf.at[b, k, pl.ds(j, 1), :],
                                      sems.at[b]).start(priority=k % 2)
            return c
        lax.fori_loop(0, tm, body, 0, unroll=4)

    b = i % 2
    pl.when(i == 0)(lambda: gather(slot_ref, 0))
    pl.when(i + 1 < n)(lambda: gather(slot_next_ref, 1 - b))
    pltpu.make_async_copy(ybuf.at[b], ybuf.at[b], sems.at[b]).wait()

    gates = gate_ref[...]
    acc = x1_ref[...]
    for k in range(TOP_K):
        acc = acc + gates[:, k:k + 1] * ybuf[b, k]
    ms = jnp.mean(acc * acc, axis=-1, keepdims=True)
    o_ref[...] = acc * lax.rsqrt(ms + EPS) * gf_ref[...]


def _combine_final(x1, gates, slot, y, gf):
    T = x1.shape[0]
    tm = FIN_TM
    n_steps = T // tm
    row_tile = lambda w: pl.BlockSpec((tm, w), lambda i: (i, 0))
    slot3 = slot.reshape(n_steps, 1, tm * TOP_K)
    slot_spec = lambda step: pl.BlockSpec((None, 1, tm * TOP_K), step, memory_space=pltpu.SMEM)
    return pl.pallas_call(
        functools.partial(_final_body, n=n_steps),
        grid=(n_steps,),
        in_specs=[
            slot_spec(lambda i: (i, 0, 0)),
            slot_spec(lambda i: (jnp.minimum(i + 1, n_steps - 1), 0, 0)),
            row_tile(D_MODEL), row_tile(LANES),
            pl.BlockSpec((1, D_MODEL), lambda i: (0, 0)),
            pl.BlockSpec(memory_space=pl.ANY),
        ],
        out_specs=row_tile(D_MODEL),
        out_shape=jax.ShapeDtypeStruct((T, D_MODEL), F32),
        scratch_shapes=[pltpu.VMEM((2, TOP_K, tm, D_MODEL), F32), pltpu.SemaphoreType.DMA((2,))],
        compiler_params=_params(("arbitrary",)),
        name="combine_final_norm",
    )(slot3, slot3, x1, gates, gf, y)


def _route(idx):
    n = idx.shape[0] * TOP_K
    n_tiles = n // MOE_TILE + N_EXPERTS
    e_flat = idx.reshape(-1)
    onehot = (e_flat[:, None] == jnp.arange(N_EXPERTS, dtype=jnp.int32)[None, :]).astype(jnp.int32)
    csum = jnp.cumsum(onehot, axis=0)
    rank = jnp.take_along_axis(csum, e_flat[:, None], axis=1)[:, 0] - 1
    counts = csum[-1]
    tiles_per_e = (counts + MOE_TILE - 1) // MOE_TILE
    tile_end = jnp.cumsum(tiles_per_e)
    tile_start = tile_end - tiles_per_e
    slot = tile_start[e_flat] * MOE_TILE + rank

    t = jnp.arange(n_tiles, dtype=jnp.int32)
    used = t < tile_end[-1]
    last_used = jnp.maximum(tile_end[-1] - 1, 0)
    tt = jnp.where(used, t, last_used)
    te = jnp.minimum(jnp.sum(tt[:, None] >= tile_end[None, :], axis=1), N_EXPERTS - 1).astype(jnp.int32)
    rows = jnp.where(used, jnp.clip(counts[te] - (tt - tile_start[te]) * MOE_TILE, 0, MOE_TILE), 0)
    nsub = ((rows + MOE_SUB - 1) // MOE_SUB).astype(jnp.int32)
    sub_end = (jnp.arange(MOE_NSUB, dtype=jnp.int32) + 1) * MOE_SUB
    notfull = (rows[:, None] < sub_end[None, :]).astype(jnp.int32).reshape(-1)
    return slot.reshape(-1, TOP_K), notfull, te, tt.astype(jnp.int32), nsub, n_tiles * MOE_TILE


def kernel(x_prompt, x_sample, norm1_g, w_in, conv_w, attn_norm_g, conv_norm_g, w_out, norm2_g,
           w_router, b_router, w_gate, b_gate, w_up, b_up, w_down, b_down, final_norm_g):
    assert norm1_g.shape[0] == 1, "one layer"
    w_in_b = w_in[0].astype(BF16)
    w_out_b = w_out[0].astype(BF16)
    wr = jnp.pad(w_router[0], ((0, 0), (0, LANES - N_EXPERTS)))
    wr_hi = wr.astype(BF16)
    wr_lo = (wr - wr_hi.astype(F32)).astype(BF16)
    br = jnp.pad(b_router[0], (0, LANES - N_EXPERTS), constant_values=-jnp.inf).reshape(1, LANES)
    g1 = norm1_g[0].reshape(1, D_MODEL)
    g2 = norm2_g[0].reshape(1, D_MODEL)
    ga = attn_norm_g[0].reshape(1, ATTN_WIDTH)
    gc = conv_norm_g[0].reshape(1, CONV_WIDTH)
    gf = final_norm_g.reshape(1, D_MODEL)

    x1s, h2s, idxs, gates = [], [], [], []
    for x in (x_prompt, x_sample):
        B, S, _ = x.shape
        x2d = x.reshape(B * S, D_MODEL)
        qkv, gates3 = _in_projection(x2d, S, g1, w_in_b, _rope_tables(S))
        attn = _band_attention(qkv, B, S, ga)
        x1, h2, idx, gate = _mixer_out(x2d, S, attn, gates3, conv_w[0], gc, w_out_b, g2, wr_hi, wr_lo, br)
        x1s.append(x1)
        h2s.append(h2)
        idxs.append(idx[:, :TOP_K])
        gates.append(gate)

    slot, notfull, te, tx, nsub, P = _route(jnp.concatenate(idxs, axis=0))
    slots = []
    t0 = 0
    for h2 in h2s:
        slots.append(slot[t0:t0 + h2.shape[0]])
        t0 += h2.shape[0]
    xs = _dispatch(h2s[0], h2s[1], slot, notfull, P)
    y = _expert_ffn(xs, te, tx, nsub, w_gate[0], b_gate[0], w_up[0], b_up[0], w_down[0], b_down[0])
    return tuple(_combine_final(x1, gate, sl, y, gf).reshape(x.shape)
                 for x, x1, gate, sl in zip((x_prompt, x_sample), x1s, gates, slots))
```

```python
import functools

import jax
import jax.numpy as jnp
from jax import lax
from jax.experimental import pallas as pl
from jax.experimental.pallas import tpu as pltpu

F32 = jnp.float32
BF16 = jnp.bfloat16

D_MODEL = 2048
HEAD_DIM = 128
N_HEADS = 8
ATTN_WIDTH = N_HEADS * HEAD_DIM
CONV_WIDTH = D_MODEL - ATTN_WIDTH
N_CONV_GROUPS = 8
QKV_WIDTH = 3 * ATTN_WIDTH
GATE_WIDTH = 3 * CONV_WIDTH
DILATED_PATTERNS = ((128, 1), (512, 4), (2048, 16))
ROPE_THETA = 500000.0
ROPE_DIM = HEAD_DIM // 4
ROPE_HALF = ROPE_DIM // 2
N_EXPERTS = 32
TOP_K = 4
D_FF = D_MODEL
SWIGLU_LIMIT = 7.0
SWIGLU_ALPHA = 1.702
EPS = 1e-5

LANES = 128
BF16_SUBLANES = 16
VMEM_LIMIT = 56 * 1024 * 1024

IN_TM = 512
IN_TN = 1024
ATTN_QB = 128
ATTN_UNROLL = 8
ATTN_FIN = 256
MIX_TM = 256
DISP_R = 256
MOE_SUB = 256
MOE_TILE = 512
MOE_NSUB = MOE_TILE // MOE_SUB
MOE_TF = 256
FIN_TM = 256


def _params(sem):
    return pltpu.CompilerParams(dimension_semantics=sem, vmem_limit_bytes=VMEM_LIMIT)


def _inproj_body(x_ref, g_ref, w_ref, cos_ref, sa_ref, sb_ref, qkv_ref, gate_ref, h_scr):
    j = pl.program_id(1)

    @pl.when(j == 0)
    def _():
        x = x_ref[...]
        ms = jnp.mean(x * x, axis=-1, keepdims=True)
        h_scr[...] = (x * lax.rsqrt(ms + EPS) * g_ref[...]).astype(BF16)

    acc = jnp.dot(h_scr[...], w_ref[...], preferred_element_type=F32)

    @pl.when(j < 2)
    def _():
        scale = jnp.where(j == 0, HEAD_DIM ** -0.5, 1.0).astype(F32)
        c, sa, sb = cos_ref[...], sa_ref[...], sb_ref[...]
        for h in range(N_HEADS):
            sl = slice(h * HEAD_DIM, (h + 1) * HEAD_DIM)
            a = acc[:, sl]
            r = (a * c + pltpu.roll(a, HEAD_DIM - ROPE_HALF, 1) * sa
                 + pltpu.roll(a, ROPE_HALF, 1) * sb)
            qkv_ref[:, sl] = r * scale

    @pl.when(j == 2)
    def _():
        qkv_ref[...] = acc

    @pl.when(j > 2)
    def _():
        gate_ref[...] = acc.astype(BF16)


def _rope_tables(S):
    inv_freq = ROPE_THETA ** (-jnp.arange(ROPE_HALF, dtype=F32) * 2.0 / ROPE_DIM)
    ang = jnp.arange(S, dtype=F32)[:, None] * inv_freq[None, :]
    cos, sin = jnp.cos(ang), jnp.sin(ang)
    pad = jnp.zeros((S, HEAD_DIM - ROPE_DIM), F32)
    zero = jnp.zeros((S, ROPE_HALF), F32)
    c = jnp.concatenate([cos, cos, pad + 1.0], axis=1)
    sa = jnp.concatenate([-sin, zero, pad], axis=1)
    sb = jnp.concatenate([zero, sin, pad], axis=1)
    return c, sa, sb


def _in_projection(x2d, S, gain, w_bf16, tables):
    T = x2d.shape[0]
    tm = IN_TM
    pos_blocks = S // tm
    n_qkv = QKV_WIDTH // IN_TN
    tab_spec = pl.BlockSpec((tm, HEAD_DIM), lambda i, j: (i % pos_blocks, 0))
    return pl.pallas_call(
        _inproj_body,
        grid=(T // tm, (QKV_WIDTH + GATE_WIDTH) // IN_TN),
        in_specs=[
            pl.BlockSpec((tm, D_MODEL), lambda i, j: (i, 0)),
            pl.BlockSpec((1, D_MODEL), lambda i, j: (0, 0)),
            pl.BlockSpec((D_MODEL, IN_TN), lambda i, j: (0, j)),
            tab_spec, tab_spec, tab_spec,
        ],
        out_specs=[
            pl.BlockSpec((tm, IN_TN), lambda i, j: (i, jnp.minimum(j, n_qkv - 1))),
            pl.BlockSpec((tm, IN_TN), lambda i, j: (i, jnp.maximum(j - n_qkv, 0))),
        ],
        out_shape=[jax.ShapeDtypeStruct((T, QKV_WIDTH), F32), jax.ShapeDtypeStruct((T, GATE_WIDTH), BF16)],
        scratch_shapes=[pltpu.VMEM((tm, D_MODEL), BF16)],
        compiler_params=_params(("arbitrary", "arbitrary")),
        name="in_projection",
    )(x2d, gain, w_bf16, *tables)


def _attn_configs(S):
    cfgs, tables = [], []
    for window, d in DILATED_PATTERNS:
        L = S // d
        qb = min(ATTN_QB, L)
        half = window // (2 * d)
        W = min(qb + 2 * half, L)
        assert L % qb == 0 and window % (2 * d) == 0
        assert (W == qb + 2 * half and L // qb >= 2) or (L == qb == W)
        if (qb, W, half) not in tables:
            tables.append((qb, W, half))
        cfgs.append((d, L, qb, W, half, tables.index((qb, W, half))))
    return cfgs, tables


def _mask_table(qb, W, half):
    rel = jnp.arange(W, dtype=jnp.int32)[None, :] - jnp.arange(qb, dtype=jnp.int32)[:, None]
    offs = jnp.array([half, 0, -half], jnp.int32)[:, None, None]
    u = rel[None] + offs
    return jnp.where((u >= 0) & (u <= 2 * half), 0.0, -1e30).astype(F32)


def _attn_body(*refs, S):
    cfgs, tables = _attn_configs(S)
    q_ref, k_ref, v_ref, gain_ref = refs[:4]
    mask_refs = refs[4:4 + len(tables)]
    o_ref, qc, kc, vc, og, lg, sc, mc, q32, k32, v32 = refs[4 + len(tables):]
    srcs, stages, f32s = (q_ref, k_ref, v_ref), (qc, kc, vc), (q32, k32, v32)
    d_prev = 1
    for g, (d, L, qb, W, half, tab) in enumerate(cfgs):
        n_blocks = L // qb
        mask_ref = mask_refs[tab]

        def rows_of_class(r, start, size, d=d):
            if d == 1:
                return pl.ds(start, size)
            return pl.ds(r + start * d, size, stride=d)

        from_prev = d_prev > 1 and d % d_prev == 0
        keep_f32 = any(c[0] > d and c[0] % d == 0 for c in cfgs) and d > 1
        step = d // d_prev if from_prev else d

        def stage(r, carry, d=d, L=L, qb=qb, from_prev=from_prev, keep_f32=keep_f32, step=step, d_prev=d_prev):
            dst = pl.ds(pl.multiple_of(r * L, qb), L)
            for src, st, f32 in zip(srcs, stages, f32s):
                if from_prev:
                    start = (r % d_prev) * (L * step) + r // d_prev
                    x = f32[pl.ds(start, L, stride=step), :]
                elif d == 1:
                    x = src[pl.ds(0, L), :]
                else:
                    x = src[pl.ds(r, L, stride=d), :]
                if keep_f32:
                    f32[dst, :] = x
                st[dst, :] = x.astype(BF16)
            return carry

        if d == 1:
            stage(0, 0)
        else:
            lax.fori_loop(0, d, stage, 0)
        d_prev = d

        def place(nb, L=L, qb=qb, half=half, W=W, n_blocks=n_blocks):
            r = nb // n_blocks
            q0 = pl.multiple_of((nb % n_blocks) * qb, qb)
            ws = jnp.clip(q0 - half, 0, L - W)
            case = jnp.where(q0 == 0, 0, jnp.where(q0 == L - qb, 2, 1))
            keys = pl.ds(pl.multiple_of(r * L + ws, BF16_SUBLANES), W)
            return r, q0, keys, case

        def scores(nb, c, g=g, L=L, qb=qb, W=W, mask_ref=mask_ref, place=place, rows_of_class=rows_of_class):
            r, q0, keys, case = place(nb)
            q = qc[pl.ds(pl.multiple_of(r * L + q0, qb), qb), :]
            s = lax.dot_general(q, kc[keys, :], (((1,), (1,)), ((), ())), preferred_element_type=F32)
            s = s + mask_ref[case]
            m = jnp.max(s, axis=-1, keepdims=True)
            sc[nb, 0:qb, 0:W] = s - m
            mc[pl.ds(pl.multiple_of(nb * qb, qb), qb), :] = jnp.broadcast_to(m, (qb, LANES))
            return c

        def values(nb, c, g=g, qb=qb, W=W, place=place, rows_of_class=rows_of_class):
            r, q0, keys, _ = place(nb)
            p = jnp.exp(sc[nb, 0:qb, 0:W])
            l = jnp.sum(p, axis=-1, keepdims=True)
            acc = jnp.dot(p.astype(BF16), vc[keys, :], preferred_element_type=F32)
            og[g, rows_of_class(r, q0, qb), :] = acc / l
            lg[g, rows_of_class(r, q0, qb), :] = mc[pl.ds(pl.multiple_of(nb * qb, qb), qb), :] + jnp.log(l)
            return c

        unroll = min(ATTN_UNROLL, S // qb)
        lax.fori_loop(0, S // qb, scores, 0, unroll=unroll)
        lax.fori_loop(0, S // qb, values, 0, unroll=unroll)

    def combine(c, carry):
        rows = pl.ds(pl.multiple_of(c * ATTN_FIN, ATTN_FIN), ATTN_FIN)
        l0, l1, l2 = lg[0, rows, :], lg[1, rows, :], lg[2, rows, :]
        mx = jnp.maximum(jnp.maximum(l0, l1), l2)
        e0, e1, e2 = jnp.exp(l0 - mx), jnp.exp(l1 - mx), jnp.exp(l2 - mx)
        a = (e0 * og[0, rows, :] + e1 * og[1, rows, :] + e2 * og[2, rows, :]) / (e0 + e1 + e2)
        ms = jnp.mean(a * a, axis=-1, keepdims=True)
        o_ref[rows, :] = (a * lax.rsqrt(ms + EPS) * gain_ref[...]).astype(BF16)
        return carry

    lax.fori_loop(0, S // ATTN_FIN, combine, 0)


def _band_attention(qkv, B, S, gain):
    cfgs, tables = _attn_configs(S)
    score_shape = (max(S // c[2] for c in cfgs), max(c[2] for c in cfgs), max(c[3] for c in cfgs))
    head = lambda t: pl.BlockSpec((S, HEAD_DIM), lambda b, h: (b, t * N_HEADS + h))
    masks = [_mask_table(*t) for t in tables]
    mask_specs = [pl.BlockSpec(m.shape, lambda b, h: (0, 0, 0)) for m in masks]
    return pl.pallas_call(
        functools.partial(_attn_body, S=S),
        grid=(B, N_HEADS),
        in_specs=[head(0), head(1), head(2), pl.BlockSpec((1, HEAD_DIM), lambda b, h: (0, h))] + mask_specs,
        out_specs=pl.BlockSpec((S, HEAD_DIM), lambda b, h: (b, h)),
        out_shape=jax.ShapeDtypeStruct((B * S, ATTN_WIDTH), BF16),
        scratch_shapes=[pltpu.VMEM((S, HEAD_DIM), BF16)] * 3
        + [pltpu.VMEM((len(DILATED_PATTERNS), S, HEAD_DIM), F32)] * 2
        + [pltpu.VMEM(score_shape, F32)] + [pltpu.VMEM((S, HEAD_DIM), F32)] * 4,
        compiler_params=_params(("arbitrary", "arbitrary")),
        name="band_attention",
    )(qkv, qkv, qkv, gain, *masks)


def _mix_body(attn_ref, cb_ref, cc_ref, cx_ref, ccp_ref, cxp_ref, ccn_ref, cxn_ref, x_ref, wout_ref,
              convw_ref, gc_ref, g2_ref, wrh_ref, wrl_ref, br_ref,
              x1_ref, h2_ref, idx_ref, gate_ref, conv_scr, *, tm, S):
    i = pl.program_id(0)

    row = lax.broadcasted_iota(jnp.int32, (tm, LANES), 0)
    r0 = i * tm
    at_start = (r0 % S) == 0
    at_end = ((r0 + tm) % S) == 0
    last = BF16_SUBLANES - 1
    gw = CONV_WIDTH // N_CONV_GROUPS
    for c in range(N_CONV_GROUPS):
        sl = slice(c * gw, (c + 1) * gw)
        u = cc_ref[:, sl].astype(F32) * cx_ref[:, sl].astype(F32)
        up = ccp_ref[last:last + 1, sl].astype(F32) * cxp_ref[last:last + 1, sl].astype(F32)
        un = ccn_ref[0:1, sl].astype(F32) * cxn_ref[0:1, sl].astype(F32)
        up = jnp.where(at_start, 0.0, up)
        un = jnp.where(at_end, 0.0, un)
        u_prev = jnp.where(row == 0, up, pltpu.roll(u, 1, 0))
        u_next = jnp.where(row == tm - 1, un, pltpu.roll(u, tm - 1, 0))
        y = convw_ref[0:1, sl] * u_prev + convw_ref[1:2, sl] * u + convw_ref[2:3, sl] * u_next
        cv = cb_ref[:, sl].astype(F32) * y
        ms = jnp.mean(cv * cv, axis=-1, keepdims=True)
        conv_scr[:, sl] = (cv * lax.rsqrt(ms + EPS) * gc_ref[:, sl]).astype(BF16)

    mix = (jnp.dot(attn_ref[...], wout_ref[0:ATTN_WIDTH, :], preferred_element_type=F32)
           + jnp.dot(conv_scr[...], wout_ref[ATTN_WIDTH:D_MODEL, :], preferred_element_type=F32))
    x1 = x_ref[...] + mix
    x1_ref[...] = x1
    ms = jnp.mean(x1 * x1, axis=-1, keepdims=True)
    h2 = x1 * lax.rsqrt(ms + EPS) * g2_ref[...]
    hi = h2.astype(BF16)
    hi_f = hi.astype(F32)
    h2_ref[...] = hi_f

    lo = (h2 - hi_f).astype(BF16)
    logits = (jnp.dot(hi, wrh_ref[...], preferred_element_type=F32)
              + jnp.dot(lo, wrh_ref[...], preferred_element_type=F32)
              + jnp.dot(hi, wrl_ref[...], preferred_element_type=F32)) + br_ref[...]

    lane = lax.broadcasted_iota(jnp.int32, (tm, LANES), 1)
    lane_f = lane.astype(F32)
    vals, ids = [], []
    for _ in range(TOP_K):
        m = jnp.max(logits, axis=-1, keepdims=True)
        ix = jnp.min(jnp.where(logits == m, lane_f, float(LANES)), axis=-1, keepdims=True)
        vals.append(m)
        ids.append(ix)
        logits = jnp.where(lane_f == ix, -jnp.inf, logits)
    es = [jnp.exp(v - vals[0]) for v in vals]
    den = es[0] + es[1] + es[2] + es[3]
    idx_tile = jnp.zeros((tm, LANES), F32)
    gate_tile = jnp.zeros((tm, LANES), F32)
    for k in range(TOP_K):
        idx_tile = jnp.where(lane == k, ids[k], idx_tile)
        gate_tile = jnp.where(lane == k, es[k] / den, gate_tile)
    idx_ref[...] = idx_tile.astype(jnp.int32)
    gate_ref[...] = gate_tile


def _mixer_out(x2d, S, attn, gates3, conv_w, gc, w_out, g2, wr_hi, wr_lo, br):
    T = x2d.shape[0]
    tm = MIX_TM
    hr = BF16_SUBLANES
    n_halo = T // hr
    row_tile = lambda w: pl.BlockSpec((tm, w), lambda i: (i, 0))
    gate_tile = lambda c: pl.BlockSpec((tm, CONV_WIDTH), lambda i: (i, c))
    prev_halo = lambda c: pl.BlockSpec((hr, CONV_WIDTH), lambda i: (jnp.maximum(i * (tm // hr) - 1, 0), c))
    next_halo = lambda c: pl.BlockSpec(
        (hr, CONV_WIDTH), lambda i: (jnp.minimum((i + 1) * (tm // hr), n_halo - 1), c))
    const = lambda shape: pl.BlockSpec(shape, lambda i: (0, 0))
    cb_col, cc_col, cx_col = 0, 1, 2
    return pl.pallas_call(
        functools.partial(_mix_body, tm=tm, S=S),
        grid=(T // tm,),
        in_specs=[
            row_tile(ATTN_WIDTH),
            gate_tile(cb_col), gate_tile(cc_col), gate_tile(cx_col),
            prev_halo(cc_col), prev_halo(cx_col), next_halo(cc_col), next_halo(cx_col),
            row_tile(D_MODEL),
            const((D_MODEL, D_MODEL)),
            const((3, CONV_WIDTH)), const((1, CONV_WIDTH)), const((1, D_MODEL)),
            const((D_MODEL, LANES)), const((D_MODEL, LANES)), const((1, LANES)),
        ],
        out_specs=[row_tile(D_MODEL), row_tile(D_MODEL), row_tile(LANES), row_tile(LANES)],
        out_shape=[
            jax.ShapeDtypeStruct((T, D_MODEL), F32),
            jax.ShapeDtypeStruct((T, D_MODEL), F32),
            jax.ShapeDtypeStruct((T, LANES), jnp.int32),
            jax.ShapeDtypeStruct((T, LANES), F32),
        ],
        scratch_shapes=[pltpu.VMEM((tm, CONV_WIDTH), BF16)],
        compiler_params=_params(("arbitrary",)),
        name="mixer_out_router",
    )(attn, gates3, gates3, gates3, gates3, gates3, gates3, gates3, x2d, w_out, conv_w,
      gc, g2, wr_hi, wr_lo, br)


def _dispatch_body(nf_ref, slot_ref, ha_ref, hb_ref, xs_hbm, zero_scr, zsem, sem, *, steps_a):
    i = pl.program_id(0)
    n_sub = xs_hbm.shape[0] // MOE_SUB

    def zero_copy(sb):
        return pltpu.make_async_copy(zero_scr, xs_hbm.at[pl.ds(sb * MOE_SUB, MOE_SUB), :], zsem)

    @pl.when(i == 0)
    def _():
        zero_scr[...] = jnp.zeros_like(zero_scr)

        def start(sb, cnt):
            pl.when(nf_ref[sb] != 0)(lambda: zero_copy(sb).start())
            return cnt + nf_ref[sb]

        cnt = lax.fori_loop(0, n_sub, start, 0)

        def wait(_, c):
            zero_copy(0).wait()
            return c

        lax.fori_loop(0, cnt, wait, 0)

    def scatter(h_ref):
        def issue(j, c):
            for k in range(TOP_K):
                s = slot_ref[0, j * TOP_K + k]
                pltpu.make_async_copy(h_ref.at[pl.ds(j, 1), :], xs_hbm.at[pl.ds(s, 1), :], sem).start(priority=k % 2)
            return c
        lax.fori_loop(0, DISP_R, issue, 0, unroll=4)

    pl.when(i < steps_a)(lambda: scatter(ha_ref))
    pl.when(i >= steps_a)(lambda: scatter(hb_ref))
    n_rows = DISP_R * TOP_K
    pltpu.make_async_copy(xs_hbm.at[pl.ds(0, n_rows), :], xs_hbm.at[pl.ds(0, n_rows), :], sem).wait()


def _dispatch(h2_a, h2_b, slot, notfull, P):
    steps_a = h2_a.shape[0] // DISP_R
    steps_b = h2_b.shape[0] // DISP_R
    n_steps = steps_a + steps_b
    grid_spec = pltpu.PrefetchScalarGridSpec(
        num_scalar_prefetch=1,
        grid=(n_steps,),
        in_specs=[
            pl.BlockSpec((None, 1, DISP_R * TOP_K), lambda i, nf: (i, 0, 0), memory_space=pltpu.SMEM),
            pl.BlockSpec((DISP_R, D_MODEL), lambda i, nf: (jnp.minimum(i, steps_a - 1), 0)),
            pl.BlockSpec((DISP_R, D_MODEL), lambda i, nf: (jnp.maximum(i - steps_a, 0), 0)),
        ],
        out_specs=pl.BlockSpec(memory_space=pl.ANY),
        scratch_shapes=[pltpu.VMEM((MOE_SUB, D_MODEL), F32), pltpu.SemaphoreType.DMA(()),
                        pltpu.SemaphoreType.DMA(())],
    )
    return pl.pallas_call(
        functools.partial(_dispatch_body, steps_a=steps_a),
        grid_spec=grid_spec,
        out_shape=jax.ShapeDtypeStruct((P, D_MODEL), F32),
        compiler_params=_params(("arbitrary",)),
        name="dispatch",
    )(notfull, slot.reshape(n_steps, 1, DISP_R * TOP_K), h2_a, h2_b)


def _expert_body(te_ref, tx_ref, ns_ref, first_ref, x_ref, bg_ref, bu_ref, bd_ref, wg_hbm, wu_hbm, wd_hbm,
                 y_ref, xb, wg_res, wu_res, wd_res, stg_g, stg_u, stg_d, sems):
    i = pl.program_id(0)
    e = te_ref[i]
    nsub = ns_ref[i]
    first = first_ref[i]
    nf = D_FF // MOE_TF
    kh = D_MODEL // 2
    fh = MOE_TF // 2

    for s in range(MOE_NSUB):
        rows = slice(s * MOE_SUB, (s + 1) * MOE_SUB)
        y_ref[rows, :] = jnp.where(s < nsub, jnp.broadcast_to(bd_ref[...], (MOE_SUB, D_MODEL)), 0.0)

    @pl.when(nsub > 0)
    def _():
        xb[...] = x_ref[...].astype(BF16)

    def sub_block(s, f):
        rows = slice(s * MOE_SUB, (s + 1) * MOE_SUB)
        x = xb[rows, :]
        g = jnp.dot(x, wg_res[f], preferred_element_type=F32) + bg_ref[f]
        u = jnp.dot(x, wu_res[f], preferred_element_type=F32) + bu_ref[f]
        g = jnp.minimum(g, SWIGLU_LIMIT)
        u = jnp.clip(u, -SWIGLU_LIMIT, SWIGLU_LIMIT)
        a = (u + 1.0) * (g * jax.nn.sigmoid(SWIGLU_ALPHA * g))
        y_ref[rows, :] += jnp.dot(a.astype(BF16), wd_res[f], preferred_element_type=F32)

    def half_copies(f, h):
        cols = pl.ds(f * MOE_TF, MOE_TF)
        drows = pl.ds(f * MOE_TF + h * fh, fh)
        return (pltpu.make_async_copy(wg_hbm.at[e, pl.ds(h * kh, kh), cols], stg_g.at[h], sems.at[0, h]),
                pltpu.make_async_copy(wu_hbm.at[e, pl.ds(h * kh, kh), cols], stg_u.at[h], sems.at[1, h]),
                pltpu.make_async_copy(wd_hbm.at[e, drows, :], stg_d.at[h], sems.at[2, h]))

    @pl.when(first == 1)
    def _():
        for h in range(2):
            for c in half_copies(0, h):
                c.start()

        def load_and_compute(f, carry):
            for h in range(2):
                for c in half_copies(f, h):
                    c.wait()
                wg_res[f, h * kh:(h + 1) * kh, :] = stg_g[h].astype(BF16)
                wu_res[f, h * kh:(h + 1) * kh, :] = stg_u[h].astype(BF16)
                wd_res[f, h * fh:(h + 1) * fh, :] = stg_d[h].astype(BF16)

                @pl.when(f + 1 < nf)
                def _():
                    for c in half_copies(f + 1, h):
                        c.start()
            for s in range(MOE_NSUB):
                pl.when(s < nsub)(functools.partial(sub_block, s, f))
            return carry

        lax.fori_loop(0, nf, load_and_compute, 0)

    @pl.when((first == 0) & (nsub == MOE_NSUB))
    def _():
        def full(f, carry):
            for s in range(MOE_NSUB):
                sub_block(s, f)
            return carry
        lax.fori_loop(0, nf, full, 0, unroll=2)

    @pl.when((first == 0) & (nsub > 0) & (nsub < MOE_NSUB))
    def _():
        def part(f, carry):
            for s in range(MOE_NSUB - 1):
                pl.when(s < nsub)(functools.partial(sub_block, s, f))
            return carry
        lax.fori_loop(0, nf, part, 0)


def _expert_ffn(xs, tile_expert, tile_xblk, tile_nsub, tile_first, w_gate, b_gate, w_up, b_up, w_down, b_down):
    P = xs.shape[0]
    n_tiles = P // MOE_TILE
    nf = D_FF // MOE_TF
    per_expert = lambda shape: pl.BlockSpec(
        (None,) + shape, lambda i, te, tx, ns, fi: (te[i],) + (0,) * len(shape))
    grid_spec = pltpu.PrefetchScalarGridSpec(
        num_scalar_prefetch=4,
        grid=(n_tiles,),
        in_specs=[
            pl.BlockSpec((MOE_TILE, D_MODEL), lambda i, te, tx, ns, fi: (tx[i], 0)),
            per_expert((nf, 1, MOE_TF)), per_expert((nf, 1, MOE_TF)), per_expert((1, D_MODEL)),
            pl.BlockSpec(memory_space=pl.ANY), pl.BlockSpec(memory_space=pl.ANY), pl.BlockSpec(memory_space=pl.ANY),
        ],
        out_specs=pl.BlockSpec((MOE_TILE, D_MODEL), lambda i, te, tx, ns, fi: (i, 0)),
        scratch_shapes=[
            pltpu.VMEM((MOE_TILE, D_MODEL), BF16),
            pltpu.VMEM((nf, D_MODEL, MOE_TF), BF16), pltpu.VMEM((nf, D_MODEL, MOE_TF), BF16),
            pltpu.VMEM((nf, MOE_TF, D_MODEL), BF16),
            pltpu.VMEM((2, D_MODEL // 2, MOE_TF), F32), pltpu.VMEM((2, D_MODEL // 2, MOE_TF), F32),
            pltpu.VMEM((2, MOE_TF // 2, D_MODEL), F32),
            pltpu.SemaphoreType.DMA((3, 2)),
        ],
    )
    return pl.pallas_call(
        _expert_body,
        grid_spec=grid_spec,
        out_shape=jax.ShapeDtypeStruct((P, D_MODEL), F32),
        compiler_params=_params(("arbitrary",)),
        name="expert_ffn",
    )(tile_expert, tile_xblk, tile_nsub, tile_first, xs,
      b_gate.reshape(N_EXPERTS, nf, 1, MOE_TF), b_up.reshape(N_EXPERTS, nf, 1, MOE_TF),
      b_down.reshape(N_EXPERTS, 1, D_MODEL), w_gate, w_up, w_down)


def _final_body(slot_ref, slot_next_ref, x1_ref, gate_ref, gf_ref, y_hbm, o_ref, ybuf, sems, *, n):
    i = pl.program_id(0)
    tm = FIN_TM

    def gather(s_ref, b):
        def body(j, c):
            for k in range(TOP_K):
                s = s_ref[0, j * TOP_K + k]
                pltpu.make_async_copy(y_hbm.at[pl.ds(s, 1), :], ybuf.at[b, k, pl.ds(j, 1), :],
                                      sems.at[b]).start(priority=k % 2)
            return c
        lax.fori_loop(0, tm, body, 0, unroll=4)

    b = i % 2
    pl.when(i == 0)(lambda: gather(slot_ref, 0))
    pl.when(i + 1 < n)(lambda: gather(slot_next_ref, 1 - b))
    pltpu.make_async_copy(ybuf.at[b], ybuf.at[b], sems.at[b]).wait()

    gates = gate_ref[...]
    acc = x1_ref[...]
    for k in range(TOP_K):
        acc = acc + gates[:, k:k + 1] * ybuf[b, k]
    ms = jnp.mean(acc * acc, axis=-1, keepdims=True)
    o_ref[...] = acc * lax.rsqrt(ms + EPS) * gf_ref[...]


def _combine_final(x1, gates, slot, y, gf):
    T = x1.shape[0]
    tm = FIN_TM
    n_steps = T // tm
    row_tile = lambda w: pl.BlockSpec((tm, w), lambda i: (i, 0))
    slot3 = slot.reshape(n_steps, 1, tm * TOP_K)
    slot_spec = lambda step: pl.BlockSpec((None, 1, tm * TOP_K), step, memory_space=pltpu.SMEM)
    return pl.pallas_call(
        functools.partial(_final_body, n=n_steps),
        grid=(n_steps,),
        in_specs=[
            slot_spec(lambda i: (i, 0, 0)),
            slot_spec(lambda i: (jnp.minimum(i + 1, n_steps - 1), 0, 0)),
            row_tile(D_MODEL), row_tile(LANES),
            pl.BlockSpec((1, D_MODEL), lambda i: (0, 0)),
            pl.BlockSpec(memory_space=pl.ANY),
        ],
        out_specs=row_tile(D_MODEL),
        out_shape=jax.ShapeDtypeStruct((T, D_MODEL), F32),
        scratch_shapes=[pltpu.VMEM((2, TOP_K, tm, D_MODEL), F32), pltpu.SemaphoreType.DMA((2,))],
        compiler_params=_params(("arbitrary",)),
        name="combine_final_norm",
    )(slot3, slot3, x1, gates, gf, y)


def _route(idx):
    n = idx.shape[0] * TOP_K
    n_tiles = n // MOE_TILE + N_EXPERTS
    e_flat = idx.reshape(-1)
    onehot = (e_flat[:, None] == jnp.arange(N_EXPERTS, dtype=jnp.int32)[None, :]).astype(jnp.int32)
    csum = jnp.cumsum(onehot, axis=0)
    rank = jnp.take_along_axis(csum, e_flat[:, None], axis=1)[:, 0] - 1
    counts = csum[-1]
    tiles_per_e = (counts + MOE_TILE - 1) // MOE_TILE
    tile_end = jnp.cumsum(tiles_per_e)
    tile_start = tile_end - tiles_per_e
    slot = tile_start[e_flat] * MOE_TILE + rank

    t = jnp.arange(n_tiles, dtype=jnp.int32)
    used = t < tile_end[-1]
    last_used = jnp.maximum(tile_end[-1] - 1, 0)
    tt = jnp.where(used, t, last_used)
    te = jnp.minimum(jnp.sum(tt[:, None] >= tile_end[None, :], axis=1), N_EXPERTS - 1).astype(jnp.int32)
    rows = jnp.where(used, jnp.clip(counts[te] - (tt - tile_start[te]) * MOE_TILE, 0, MOE_TILE), 0)
    nsub = ((rows + MOE_SUB - 1) // MOE_SUB).astype(jnp.int32)
    sub_end = (jnp.arange(MOE_NSUB, dtype=jnp.int32) + 1) * MOE_SUB
    notfull = (rows[:, None] < sub_end[None, :]).astype(jnp.int32).reshape(-1)
    first = (used & (t == tile_start[te])).astype(jnp.int32)
    return slot.reshape(-1, TOP_K), notfull, te, tt.astype(jnp.int32), nsub, first, n_tiles * MOE_TILE


def kernel(x_prompt, x_sample, norm1_g, w_in, conv_w, attn_norm_g, conv_norm_g, w_out, norm2_g,
           w_router, b_router, w_gate, b_gate, w_up, b_up, w_down, b_down, final_norm_g):
    assert norm1_g.shape[0] == 1, "one layer"
    w_in_b = w_in[0].astype(BF16)
    w_out_b = w_out[0].astype(BF16)
    wr = jnp.pad(w_router[0], ((0, 0), (0, LANES - N_EXPERTS)))
    wr_hi = wr.astype(BF16)
    wr_lo = (wr - wr_hi.astype(F32)).astype(BF16)
    br = jnp.pad(b_router[0], (0, LANES - N_EXPERTS), constant_values=-jnp.inf).reshape(1, LANES)
    g1 = norm1_g[0].reshape(1, D_MODEL)
    g2 = norm2_g[0].reshape(1, D_MODEL)
    ga = attn_norm_g[0].reshape(1, ATTN_WIDTH)
    gc = conv_norm_g[0].reshape(1, CONV_WIDTH)
    gf = final_norm_g.reshape(1, D_MODEL)

    x1s, h2s, idxs, gates = [], [], [], []
    for x in (x_prompt, x_sample):
        B, S, _ = x.shape
        x2d = x.reshape(B * S, D_MODEL)
        qkv, gates3 = _in_projection(x2d, S, g1, w_in_b, _rope_tables(S))
        attn = _band_attention(qkv, B, S, ga)
        x1, h2, idx, gate = _mixer_out(x2d, S, attn, gates3, conv_w[0], gc, w_out_b, g2, wr_hi, wr_lo, br)
        x1s.append(x1)
        h2s.append(h2)
        idxs.append(idx[:, :TOP_K])
        gates.append(gate)

    slot, notfull, te, tx, nsub, first, P = _route(jnp.concatenate(idxs, axis=0))
    slots = []
    t0 = 0
    for h2 in h2s:
        slots.append(slot[t0:t0 + h2.shape[0]])
        t0 += h2.shape[0]
    xs = _dispatch(h2s[0], h2s[1], slot, notfull, P)
    y = _expert_ffn(xs, te, tx, nsub, first, w_gate[0], b_gate[0], w_up[0], b_up[0], w_down[0], b_down[0])
    return tuple(_combine_final(x1, gate, sl, y, gf).reshape(x.shape)
                 for x, x1, gate, sl in zip((x_prompt, x_sample), x1s, gates, slots))
```

```python
import functools

import jax
import jax.numpy as jnp
from jax import lax
from jax.experimental import pallas as pl
from jax.experimental.pallas import tpu as pltpu

F32 = jnp.float32
BF16 = jnp.bfloat16

D_MODEL = 2048
HEAD_DIM = 128
N_HEADS = 8
ATTN_WIDTH = N_HEADS * HEAD_DIM
CONV_WIDTH = D_MODEL - ATTN_WIDTH
N_CONV_GROUPS = 8
QKV_WIDTH = 3 * ATTN_WIDTH
GATE_WIDTH = 3 * CONV_WIDTH
DILATED_PATTERNS = ((128, 1), (512, 4), (2048, 16))
ROPE_THETA = 500000.0
ROPE_DIM = HEAD_DIM // 4
ROPE_HALF = ROPE_DIM // 2
N_EXPERTS = 32
TOP_K = 4
D_FF = D_MODEL
SWIGLU_LIMIT = 7.0
SWIGLU_ALPHA = 1.702
EPS = 1e-5

LANES = 128
BF16_SUBLANES = 16
VMEM_LIMIT = 56 * 1024 * 1024

IN_TM = 512
IN_TN = 1024
ATTN_QB = 128
ATTN_UNROLL = 8
ATTN_FIN = 256
MIX_TM = 256
DISP_R = 256
MOE_SUB = 256
MOE_TILE = 512
MOE_NSUB = MOE_TILE // MOE_SUB
MOE_TF = 256
FIN_TM = 256


def _params(sem):
    return pltpu.CompilerParams(dimension_semantics=sem, vmem_limit_bytes=VMEM_LIMIT)


def _normed(x_ref, g_ref, h_scr):
    x = x_ref[...]
    ms = jnp.mean(x * x, axis=-1, keepdims=True)
    h_scr[...] = (x * lax.rsqrt(ms + EPS) * g_ref[...]).astype(BF16)


def _qkv_body(x_ref, g_ref, w_ref, c_ref, sa_ref, sb_ref, o_ref, h_scr):
    pl.when(pl.program_id(1) == 0)(functools.partial(_normed, x_ref, g_ref, h_scr))
    acc = jnp.dot(h_scr[...], w_ref[...], preferred_element_type=F32)
    c, sa, sb = c_ref[...], sa_ref[...], sb_ref[...]
    for h in range(N_HEADS):
        sl = slice(h * HEAD_DIM, (h + 1) * HEAD_DIM)
        a = acc[:, sl]
        o_ref[:, sl] = a * c + pltpu.roll(a, HEAD_DIM - ROPE_HALF, 1) * sa + pltpu.roll(a, ROPE_HALF, 1) * sb


def _gates_body(x_ref, g_ref, w_ref, o_ref, h_scr):
    pl.when(pl.program_id(1) == 0)(functools.partial(_normed, x_ref, g_ref, h_scr))
    o_ref[...] = jnp.dot(h_scr[...], w_ref[...], preferred_element_type=F32).astype(BF16)


def _rope_tables(S):
    inv_freq = ROPE_THETA ** (-jnp.arange(ROPE_HALF, dtype=F32) * 2.0 / ROPE_DIM)
    ang = jnp.arange(S, dtype=F32)[:, None] * inv_freq[None, :]
    cos, sin = jnp.cos(ang), jnp.sin(ang)
    pad = jnp.zeros((S, HEAD_DIM - ROPE_DIM), F32)
    zero = jnp.zeros((S, ROPE_HALF), F32)
    c = jnp.concatenate([cos, cos, pad + 1.0], axis=1)
    sa = jnp.concatenate([-sin, zero, pad], axis=1)
    sb = jnp.concatenate([zero, sin, pad], axis=1)
    scale = HEAD_DIM ** -0.5
    none = jnp.zeros_like(c)
    return (jnp.stack([c * scale, c, none + 1.0]), jnp.stack([sa * scale, sa, none]),
            jnp.stack([sb * scale, sb, none]))


def _in_projection(x2d, S, gain, w_bf16, tables):
    T = x2d.shape[0]
    tm = IN_TM
    pos_blocks = S // tm
    n_qkv = QKV_WIDTH // IN_TN
    x_spec = pl.BlockSpec((tm, D_MODEL), lambda i, j: (i, 0))
    g_spec = pl.BlockSpec((1, D_MODEL), lambda i, j: (0, 0))
    out_spec = pl.BlockSpec((tm, IN_TN), lambda i, j: (i, j))
    tab_spec = pl.BlockSpec((None, tm, HEAD_DIM), lambda i, j: (j, i % pos_blocks, 0))
    common = dict(scratch_shapes=[pltpu.VMEM((tm, D_MODEL), BF16)],
                  compiler_params=_params(("arbitrary", "arbitrary")))
    qkv = pl.pallas_call(
        _qkv_body,
        grid=(T // tm, n_qkv),
        in_specs=[x_spec, g_spec, pl.BlockSpec((D_MODEL, IN_TN), lambda i, j: (0, j)), tab_spec, tab_spec, tab_spec],
        out_specs=out_spec,
        out_shape=jax.ShapeDtypeStruct((T, QKV_WIDTH), F32),
        name="qkv_projection", **common,
    )(x2d, gain, w_bf16, *tables)
    gates = pl.pallas_call(
        _gates_body,
        grid=(T // tm, GATE_WIDTH // IN_TN),
        in_specs=[x_spec, g_spec, pl.BlockSpec((D_MODEL, IN_TN), lambda i, j: (0, n_qkv + j))],
        out_specs=out_spec,
        out_shape=jax.ShapeDtypeStruct((T, GATE_WIDTH), BF16),
        name="gate_projection", **common,
    )(x2d, gain, w_bf16)
    return qkv, gates


def _attn_configs(S):
    cfgs, tables = [], []
    for window, d in DILATED_PATTERNS:
        L = S // d
        qb = min(ATTN_QB, L)
        half = window // (2 * d)
        W = min(qb + 2 * half, L)
        assert L % qb == 0 and window % (2 * d) == 0
        assert (W == qb + 2 * half and L // qb >= 2) or (L == qb == W)
        if (qb, W, half) not in tables:
            tables.append((qb, W, half))
        cfgs.append((d, L, qb, W, half, tables.index((qb, W, half))))
    return cfgs, tables


def _mask_table(qb, W, half):
    rel = jnp.arange(W, dtype=jnp.int32)[None, :] - jnp.arange(qb, dtype=jnp.int32)[:, None]
    offs = jnp.array([half, 0, -half], jnp.int32)[:, None, None]
    u = rel[None] + offs
    return jnp.where((u >= 0) & (u <= 2 * half), 0.0, -1e30).astype(F32)


def _attn_body(*refs, S):
    cfgs, tables = _attn_configs(S)
    q_ref, k_ref, v_ref, gain_ref = refs[:4]
    mask_refs = refs[4:4 + len(tables)]
    o_ref, qc, kc, vc, og, lg, sc, mc, q32, k32, v32 = refs[4 + len(tables):]
    srcs, stages, f32s = (q_ref, k_ref, v_ref), (qc, kc, vc), (q32, k32, v32)
    d_prev = 1
    for g, (d, L, qb, W, half, tab) in enumerate(cfgs):
        n_blocks = L // qb
        mask_ref = mask_refs[tab]

        def rows_of_class(r, start, size, d=d):
            if d == 1:
                return pl.ds(start, size)
            return pl.ds(r + start * d, size, stride=d)

        from_prev = d_prev > 1 and d % d_prev == 0
        keep_f32 = any(c[0] > d and c[0] % d == 0 for c in cfgs) and d > 1
        step = d // d_prev if from_prev else d

        def stage(r, carry, d=d, L=L, qb=qb, from_prev=from_prev, keep_f32=keep_f32, step=step, d_prev=d_prev):
            dst = pl.ds(pl.multiple_of(r * L, qb), L)
            for src, st, f32 in zip(srcs, stages, f32s):
                if from_prev:
                    start = (r % d_prev) * (L * step) + r // d_prev
                    x = f32[pl.ds(start, L, stride=step), :]
                elif d == 1:
                    x = src[pl.ds(0, L), :]
                else:
                    x = src[pl.ds(r, L, stride=d), :]
                if keep_f32:
                    f32[dst, :] = x
                st[dst, :] = x.astype(BF16)
            return carry

        if d == 1:
            stage(0, 0)
        else:
            lax.fori_loop(0, d, stage, 0)
        d_prev = d

        def place(nb, L=L, qb=qb, half=half, W=W, n_blocks=n_blocks):
            r = nb // n_blocks
            q0 = pl.multiple_of((nb % n_blocks) * qb, qb)
            ws = jnp.clip(q0 - half, 0, L - W)
            case = jnp.where(q0 == 0, 0, jnp.where(q0 == L - qb, 2, 1))
            keys = pl.ds(pl.multiple_of(r * L + ws, BF16_SUBLANES), W)
            return r, q0, keys, case

        def scores(nb, c, g=g, L=L, qb=qb, W=W, mask_ref=mask_ref, place=place, rows_of_class=rows_of_class):
            r, q0, keys, case = place(nb)
            q = qc[pl.ds(pl.multiple_of(r * L + q0, qb), qb), :]
            s = lax.dot_general(q, kc[keys, :], (((1,), (1,)), ((), ())), preferred_element_type=F32)
            s = s + mask_ref[case]
            m = jnp.max(s, axis=-1, keepdims=True)
            sc[nb, 0:qb, 0:W] = s - m
            mc[pl.ds(pl.multiple_of(nb * qb, qb), qb), :] = jnp.broadcast_to(m, (qb, LANES))
            return c

        def values(nb, c, g=g, qb=qb, W=W, place=place, rows_of_class=rows_of_class):
            r, q0, keys, _ = place(nb)
            p = jnp.exp(sc[nb, 0:qb, 0:W])
            l = jnp.sum(p, axis=-1, keepdims=True)
            acc = jnp.dot(p.astype(BF16), vc[keys, :], preferred_element_type=F32)
            og[g, rows_of_class(r, q0, qb), :] = acc / l
            lg[g, rows_of_class(r, q0, qb), :] = mc[pl.ds(pl.multiple_of(nb * qb, qb), qb), :] + jnp.log(l)
            return c

        unroll = min(ATTN_UNROLL, S // qb)
        lax.fori_loop(0, S // qb, scores, 0, unroll=unroll)
        lax.fori_loop(0, S // qb, values, 0, unroll=unroll)

    def combine(c, carry):
        rows = pl.ds(pl.multiple_of(c * ATTN_FIN, ATTN_FIN), ATTN_FIN)
        l0, l1, l2 = lg[0, rows, :], lg[1, rows, :], lg[2, rows, :]
        mx = jnp.maximum(jnp.maximum(l0, l1), l2)
        e0, e1, e2 = jnp.exp(l0 - mx), jnp.exp(l1 - mx), jnp.exp(l2 - mx)
        a = (e0 * og[0, rows, :] + e1 * og[1, rows, :] + e2 * og[2, rows, :]) / (e0 + e1 + e2)
        ms = jnp.mean(a * a, axis=-1, keepdims=True)
        o_ref[rows, :] = (a * lax.rsqrt(ms + EPS) * gain_ref[...]).astype(BF16)
        return carry

    lax.fori_loop(0, S // ATTN_FIN, combine, 0)


def _band_attention(qkv, B, S, gain):
    cfgs, tables = _attn_configs(S)
    score_shape = (max(S // c[2] for c in cfgs), max(c[2] for c in cfgs), max(c[3] for c in cfgs))
    head = lambda t: pl.BlockSpec((S, HEAD_DIM), lambda b, h: (b, t * N_HEADS + h))
    masks = [_mask_table(*t) for t in tables]
    mask_specs = [pl.BlockSpec(m.shape, lambda b, h: (0, 0, 0)) for m in masks]
    return pl.pallas_call(
        functools.partial(_attn_body, S=S),
        grid=(B, N_HEADS),
        in_specs=[head(0), head(1), head(2), pl.BlockSpec((1, HEAD_DIM), lambda b, h: (0, h))] + mask_specs,
        out_specs=pl.BlockSpec((S, HEAD_DIM), lambda b, h: (b, h)),
        out_shape=jax.ShapeDtypeStruct((B * S, ATTN_WIDTH), BF16),
        scratch_shapes=[pltpu.VMEM((S, HEAD_DIM), BF16)] * 3
        + [pltpu.VMEM((len(DILATED_PATTERNS), S, HEAD_DIM), F32)] * 2
        + [pltpu.VMEM(score_shape, F32)] + [pltpu.VMEM((S, HEAD_DIM), F32)] * 4,
        compiler_params=_params(("arbitrary", "arbitrary")),
        name="band_attention",
    )(qkv, qkv, qkv, gain, *masks)


def _mix_body(attn_ref, cb_ref, cc_ref, cx_ref, ccp_ref, cxp_ref, ccn_ref, cxn_ref, x_ref, wout_ref,
              convw_ref, gc_ref, g2_ref, wrh_ref, wrl_ref, br_ref,
              x1_ref, h2_ref, idx_ref, gate_ref, conv_scr, *, tm, S):
    i = pl.program_id(0)

    row = lax.broadcasted_iota(jnp.int32, (tm, LANES), 0)
    r0 = i * tm
    at_start = (r0 % S) == 0
    at_end = ((r0 + tm) % S) == 0
    last = BF16_SUBLANES - 1
    gw = CONV_WIDTH // N_CONV_GROUPS
    for c in range(N_CONV_GROUPS):
        sl = slice(c * gw, (c + 1) * gw)
        u = cc_ref[:, sl].astype(F32) * cx_ref[:, sl].astype(F32)
        up = ccp_ref[last:last + 1, sl].astype(F32) * cxp_ref[last:last + 1, sl].astype(F32)
        un = ccn_ref[0:1, sl].astype(F32) * cxn_ref[0:1, sl].astype(F32)
        up = jnp.where(at_start, 0.0, up)
        un = jnp.where(at_end, 0.0, un)
        u_prev = jnp.where(row == 0, up, pltpu.roll(u, 1, 0))
        u_next = jnp.where(row == tm - 1, un, pltpu.roll(u, tm - 1, 0))
        y = convw_ref[0:1, sl] * u_prev + convw_ref[1:2, sl] * u + convw_ref[2:3, sl] * u_next
        cv = cb_ref[:, sl].astype(F32) * y
        ms = jnp.mean(cv * cv, axis=-1, keepdims=True)
        conv_scr[:, sl] = (cv * lax.rsqrt(ms + EPS) * gc_ref[:, sl]).astype(BF16)

    mix = (jnp.dot(attn_ref[...], wout_ref[0:ATTN_WIDTH, :], preferred_element_type=F32)
           + jnp.dot(conv_scr[...], wout_ref[ATTN_WIDTH:D_MODEL, :], preferred_element_type=F32))
    x1 = x_ref[...] + mix
    x1_ref[...] = x1
    ms = jnp.mean(x1 * x1, axis=-1, keepdims=True)
    h2 = x1 * lax.rsqrt(ms + EPS) * g2_ref[...]
    hi = h2.astype(BF16)
    hi_f = hi.astype(F32)
    h2_ref[...] = hi_f

    lo = (h2 - hi_f).astype(BF16)
    logits = (jnp.dot(hi, wrh_ref[...], preferred_element_type=F32)
              + jnp.dot(lo, wrh_ref[...], preferred_element_type=F32)
              + jnp.dot(hi, wrl_ref[...], preferred_element_type=F32)) + br_ref[...]

    lane = lax.broadcasted_iota(jnp.int32, (tm, LANES), 1)
    lane_f = lane.astype(F32)
    vals, ids = [], []
    for _ in range(TOP_K):
        m = jnp.max(logits, axis=-1, keepdims=True)
        ix = jnp.min(jnp.where(logits == m, lane_f, float(LANES)), axis=-1, keepdims=True)
        vals.append(m)
        ids.append(ix)
        logits = jnp.where(lane_f == ix, -jnp.inf, logits)
    es = [jnp.exp(v - vals[0]) for v in vals]
    den = es[0] + es[1] + es[2] + es[3]
    idx_tile = jnp.zeros((tm, LANES), F32)
    gate_tile = jnp.zeros((tm, LANES), F32)
    for k in range(TOP_K):
        idx_tile = jnp.where(lane == k, ids[k], idx_tile)
        gate_tile = jnp.where(lane == k, es[k] / den, gate_tile)
    idx_ref[...] = idx_tile.astype(jnp.int32)
    gate_ref[...] = gate_tile


def _mixer_out(x2d, S, attn, gates3, conv_w, gc, w_out, g2, wr_hi, wr_lo, br):
    T = x2d.shape[0]
    tm = MIX_TM
    hr = BF16_SUBLANES
    n_halo = T // hr
    row_tile = lambda w: pl.BlockSpec((tm, w), lambda i: (i, 0))
    gate_tile = lambda c: pl.BlockSpec((tm, CONV_WIDTH), lambda i: (i, c))
    prev_halo = lambda c: pl.BlockSpec((hr, CONV_WIDTH), lambda i: (jnp.maximum(i * (tm // hr) - 1, 0), c))
    next_halo = lambda c: pl.BlockSpec(
        (hr, CONV_WIDTH), lambda i: (jnp.minimum((i + 1) * (tm // hr), n_halo - 1), c))
    const = lambda shape: pl.BlockSpec(shape, lambda i: (0, 0))
    cb_col, cc_col, cx_col = 0, 1, 2
    return pl.pallas_call(
        functools.partial(_mix_body, tm=tm, S=S),
        grid=(T // tm,),
        in_specs=[
            row_tile(ATTN_WIDTH),
            gate_tile(cb_col), gate_tile(cc_col), gate_tile(cx_col),
            prev_halo(cc_col), prev_halo(cx_col), next_halo(cc_col), next_halo(cx_col),
            row_tile(D_MODEL),
            const((D_MODEL, D_MODEL)),
            const((3, CONV_WIDTH)), const((1, CONV_WIDTH)), const((1, D_MODEL)),
            const((D_MODEL, LANES)), const((D_MODEL, LANES)), const((1, LANES)),
        ],
        out_specs=[row_tile(D_MODEL), row_tile(D_MODEL), row_tile(LANES), row_tile(LANES)],
        out_shape=[
            jax.ShapeDtypeStruct((T, D_MODEL), F32),
            jax.ShapeDtypeStruct((T, D_MODEL), F32),
            jax.ShapeDtypeStruct((T, LANES), jnp.int32),
            jax.ShapeDtypeStruct((T, LANES), F32),
        ],
        scratch_shapes=[pltpu.VMEM((tm, CONV_WIDTH), BF16)],
        compiler_params=_params(("arbitrary",)),
        name="mixer_out_router",
    )(attn, gates3, gates3, gates3, gates3, gates3, gates3, gates3, x2d, w_out, conv_w,
      gc, g2, wr_hi, wr_lo, br)


def _dispatch_body(nf_ref, slot_ref, ha_ref, hb_ref, xs_hbm, zero_scr, zsem, sem, *, steps_a):
    i = pl.program_id(0)
    n_sub = xs_hbm.shape[0] // MOE_SUB

    def zero_copy(sb):
        return pltpu.make_async_copy(zero_scr, xs_hbm.at[pl.ds(sb * MOE_SUB, MOE_SUB), :], zsem)

    @pl.when(i == 0)
    def _():
        zero_scr[...] = jnp.zeros_like(zero_scr)

        def start(sb, cnt):
            pl.when(nf_ref[sb] != 0)(lambda: zero_copy(sb).start())
            return cnt + nf_ref[sb]

        cnt = lax.fori_loop(0, n_sub, start, 0)

        def wait(_, c):
            zero_copy(0).wait()
            return c

        lax.fori_loop(0, cnt, wait, 0)

    def scatter(h_ref):
        def issue(j, c):
            for k in range(TOP_K):
                s = slot_ref[0, j * TOP_K + k]
                pltpu.make_async_copy(h_ref.at[pl.ds(j, 1), :], xs_hbm.at[pl.ds(s, 1), :], sem).start(priority=k % 2)
            return c
        lax.fori_loop(0, DISP_R, issue, 0, unroll=4)

    pl.when(i < steps_a)(lambda: scatter(ha_ref))
    pl.when(i >= steps_a)(lambda: scatter(hb_ref))
    n_rows = DISP_R * TOP_K
    pltpu.make_async_copy(xs_hbm.at[pl.ds(0, n_rows), :], xs_hbm.at[pl.ds(0, n_rows), :], sem).wait()


def _dispatch(h2_a, h2_b, slot, notfull, P):
    steps_a = h2_a.shape[0] // DISP_R
    steps_b = h2_b.shape[0] // DISP_R
    n_steps = steps_a + steps_b
    grid_spec = pltpu.PrefetchScalarGridSpec(
        num_scalar_prefetch=1,
        grid=(n_steps,),
        in_specs=[
            pl.BlockSpec((None, 1, DISP_R * TOP_K), lambda i, nf: (i, 0, 0), memory_space=pltpu.SMEM),
            pl.BlockSpec((DISP_R, D_MODEL), lambda i, nf: (jnp.minimum(i, steps_a - 1), 0)),
            pl.BlockSpec((DISP_R, D_MODEL), lambda i, nf: (jnp.maximum(i - steps_a, 0), 0)),
        ],
        out_specs=pl.BlockSpec(memory_space=pl.ANY),
        scratch_shapes=[pltpu.VMEM((MOE_SUB, D_MODEL), F32), pltpu.SemaphoreType.DMA(()),
                        pltpu.SemaphoreType.DMA(())],
    )
    return pl.pallas_call(
        functools.partial(_dispatch_body, steps_a=steps_a),
        grid_spec=grid_spec,
        out_shape=jax.ShapeDtypeStruct((P, D_MODEL), F32),
        compiler_params=_params(("arbitrary",)),
        name="dispatch",
    )(notfull, slot.reshape(n_steps, 1, DISP_R * TOP_K), h2_a, h2_b)


def _expert_body(te_ref, tx_ref, ns_ref, first_ref, x_ref, bg_ref, bu_ref, bd_ref, wg_hbm, wu_hbm, wd_hbm,
                 y_ref, xb, wg_res, wu_res, wd_res, stg_g, stg_u, stg_d, sems):
    i = pl.program_id(0)
    e = te_ref[i]
    nsub = ns_ref[i]
    first = first_ref[i]
    nf = D_FF // MOE_TF
    kh = D_MODEL // 2
    fh = MOE_TF // 2

    for s in range(MOE_NSUB):
        rows = slice(s * MOE_SUB, (s + 1) * MOE_SUB)
        y_ref[rows, :] = jnp.where(s < nsub, jnp.broadcast_to(bd_ref[...], (MOE_SUB, D_MODEL)), 0.0)

    @pl.when(nsub > 0)
    def _():
        xb[...] = x_ref[...].astype(BF16)

    def sub_block(s, f):
        rows = slice(s * MOE_SUB, (s + 1) * MOE_SUB)
        x = xb[rows, :]
        g = jnp.dot(x, wg_res[f], preferred_element_type=F32) + bg_ref[f]
        u = jnp.dot(x, wu_res[f], preferred_element_type=F32) + bu_ref[f]
        g = jnp.minimum(g, SWIGLU_LIMIT)
        u = jnp.clip(u, -SWIGLU_LIMIT, SWIGLU_LIMIT)
        a = (u + 1.0) * (g * jax.nn.sigmoid(SWIGLU_ALPHA * g))
        y_ref[rows, :] += jnp.dot(a.astype(BF16), wd_res[f], preferred_element_type=F32)

    def half_copies(f, h):
        cols = pl.ds(f * MOE_TF, MOE_TF)
        drows = pl.ds(f * MOE_TF + h * fh, fh)
        return (pltpu.make_async_copy(wg_hbm.at[e, pl.ds(h * kh, kh), cols], stg_g.at[h], sems.at[0, h]),
                pltpu.make_async_copy(wu_hbm.at[e, pl.ds(h * kh, kh), cols], stg_u.at[h], sems.at[1, h]),
                pltpu.make_async_copy(wd_hbm.at[e, drows, :], stg_d.at[h], sems.at[2, h]))

    @pl.when(first == 1)
    def _():
        for h in range(2):
            for c in half_copies(0, h):
                c.start()

        def load_and_compute(f, carry):
            for h in range(2):
                for c in half_copies(f, h):
                    c.wait()
                wg_res[f, h * kh:(h + 1) * kh, :] = stg_g[h].astype(BF16)
                wu_res[f, h * kh:(h + 1) * kh, :] = stg_u[h].astype(BF16)
                wd_res[f, h * fh:(h + 1) * fh, :] = stg_d[h].astype(BF16)

                @pl.when(f + 1 < nf)
                def _():
                    for c in half_copies(f + 1, h):
                        c.start()
            for s in range(MOE_NSUB):
                pl.when(s < nsub)(functools.partial(sub_block, s, f))
            return carry

        lax.fori_loop(0, nf, load_and_compute, 0)

    @pl.when((first == 0) & (nsub == MOE_NSUB))
    def _():
        def full(f, carry):
            for s in range(MOE_NSUB):
                sub_block(s, f)
            return carry
        lax.fori_loop(0, nf, full, 0, unroll=4)

    @pl.when((first == 0) & (nsub > 0) & (nsub < MOE_NSUB))
    def _():
        def part(f, carry):
            for s in range(MOE_NSUB - 1):
                pl.when(s < nsub)(functools.partial(sub_block, s, f))
            return carry
        lax.fori_loop(0, nf, part, 0)


def _expert_ffn(xs, tile_expert, tile_xblk, tile_nsub, tile_first, w_gate, b_gate, w_up, b_up, w_down, b_down):
    P = xs.shape[0]
    n_tiles = P // MOE_TILE
    nf = D_FF // MOE_TF
    per_expert = lambda shape: pl.BlockSpec(
        (None,) + shape, lambda i, te, tx, ns, fi: (te[i],) + (0,) * len(shape))
    grid_spec = pltpu.PrefetchScalarGridSpec(
        num_scalar_prefetch=4,
        grid=(n_tiles,),
        in_specs=[
            pl.BlockSpec((MOE_TILE, D_MODEL), lambda i, te, tx, ns, fi: (tx[i], 0)),
            per_expert((nf, 1, MOE_TF)), per_expert((nf, 1, MOE_TF)), per_expert((1, D_MODEL)),
            pl.BlockSpec(memory_space=pl.ANY), pl.BlockSpec(memory_space=pl.ANY), pl.BlockSpec(memory_space=pl.ANY),
        ],
        out_specs=pl.BlockSpec((MOE_TILE, D_MODEL), lambda i, te, tx, ns, fi: (i, 0)),
        scratch_shapes=[
            pltpu.VMEM((MOE_TILE, D_MODEL), BF16),
            pltpu.VMEM((nf, D_MODEL, MOE_TF), BF16), pltpu.VMEM((nf, D_MODEL, MOE_TF), BF16),
            pltpu.VMEM((nf, MOE_TF, D_MODEL), BF16),
            pltpu.VMEM((2, D_MODEL // 2, MOE_TF), F32), pltpu.VMEM((2, D_MODEL // 2, MOE_TF), F32),
            pltpu.VMEM((2, MOE_TF // 2, D_MODEL), F32),
            pltpu.SemaphoreType.DMA((3, 2)),
        ],
    )
    return pl.pallas_call(
        _expert_body,
        grid_spec=grid_spec,
        out_shape=jax.ShapeDtypeStruct((P, D_MODEL), F32),
        compiler_params=_params(("arbitrary",)),
        name="expert_ffn",
    )(tile_expert, tile_xblk, tile_nsub, tile_first, xs,
      b_gate.reshape(N_EXPERTS, nf, 1, MOE_TF), b_up.reshape(N_EXPERTS, nf, 1, MOE_TF),
      b_down.reshape(N_EXPERTS, 1, D_MODEL), w_gate, w_up, w_down)


def _final_body(slot_ref, slot_next_ref, x1_ref, gate_ref, gf_ref, y_hbm, o_ref, ybuf, sems, *, n):
    i = pl.program_id(0)
    tm = FIN_TM

    def gather(s_ref, b):
        def body(j, c):
            for k in range(TOP_K):
                s = s_ref[0, j * TOP_K + k]
                pltpu.make_async_copy(y_hbm.at[pl.ds(s, 1), :], ybuf.at[b, k, pl.ds(j, 1), :],
                                      sems.at[b]).start(priority=k % 2)
            return c
        lax.fori_loop(0, tm, body, 0, unroll=8)

    b = i % 2
    pl.when(i == 0)(lambda: gather(slot_ref, 0))
    pl.when((i + 1 < n) & (b == 0))(lambda: gather(slot_next_ref, 1))
    pl.when((i + 1 < n) & (b == 1))(lambda: gather(slot_next_ref, 0))
    pltpu.make_async_copy(ybuf.at[b], ybuf.at[b], sems.at[b]).wait()

    gates = gate_ref[...]
    acc = x1_ref[...]
    for k in range(TOP_K):
        acc = acc + gates[:, k:k + 1] * ybuf[b, k]
    ms = jnp.mean(acc * acc, axis=-1, keepdims=True)
    o_ref[...] = acc * lax.rsqrt(ms + EPS) * gf_ref[...]


def _combine_final(x1, gates, slot, y, gf):
    T = x1.shape[0]
    tm = FIN_TM
    n_steps = T // tm
    row_tile = lambda w: pl.BlockSpec((tm, w), lambda i: (i, 0))
    slot3 = slot.reshape(n_steps, 1, tm * TOP_K)
    slot_spec = lambda step: pl.BlockSpec((None, 1, tm * TOP_K), step, memory_space=pltpu.SMEM)
    return pl.pallas_call(
        functools.partial(_final_body, n=n_steps),
        grid=(n_steps,),
        in_specs=[
            slot_spec(lambda i: (i, 0, 0)),
            slot_spec(lambda i: (jnp.minimum(i + 1, n_steps - 1), 0, 0)),
            row_tile(D_MODEL), row_tile(LANES),
            pl.BlockSpec((1, D_MODEL), lambda i: (0, 0)),
            pl.BlockSpec(memory_space=pl.ANY),
        ],
        out_specs=row_tile(D_MODEL),
        out_shape=jax.ShapeDtypeStruct((T, D_MODEL), F32),
        scratch_shapes=[pltpu.VMEM((2, TOP_K, tm, D_MODEL), F32), pltpu.SemaphoreType.DMA((2,))],
        compiler_params=_params(("arbitrary",)),
        name="combine_final_norm",
    )(slot3, slot3, x1, gates, gf, y)


def _route(idx):
    n = idx.shape[0] * TOP_K
    n_tiles = n // MOE_TILE + N_EXPERTS
    e_flat = idx.reshape(-1)
    onehot = (e_flat[:, None] == jnp.arange(N_EXPERTS, dtype=jnp.int32)[None, :]).astype(jnp.int32)
    csum = jnp.cumsum(onehot, axis=0)
    rank = jnp.take_along_axis(csum, e_flat[:, None], axis=1)[:, 0] - 1
    counts = csum[-1]
    tiles_per_e = (counts + MOE_TILE - 1) // MOE_TILE
    tile_end = jnp.cumsum(tiles_per_e)
    tile_start = tile_end - tiles_per_e
    slot = tile_start[e_flat] * MOE_TILE + rank

    t = jnp.arange(n_tiles, dtype=jnp.int32)
    used = t < tile_end[-1]
    last_used = jnp.maximum(tile_end[-1] - 1, 0)
    tt = jnp.where(used, t, last_used)
    te = jnp.minimum(jnp.sum(tt[:, None] >= tile_end[None, :], axis=1), N_EXPERTS - 1).astype(jnp.int32)
    rows = jnp.where(used, jnp.clip(counts[te] - (tt - tile_start[te]) * MOE_TILE, 0, MOE_TILE), 0)
    nsub = ((rows + MOE_SUB - 1) // MOE_SUB).astype(jnp.int32)
    sub_end = (jnp.arange(MOE_NSUB, dtype=jnp.int32) + 1) * MOE_SUB
    notfull = (rows[:, None] < sub_end[None, :]).astype(jnp.int32).reshape(-1)
    first = (used & (t == tile_start[te])).astype(jnp.int32)
    return slot.reshape(-1, TOP_K), notfull, te, tt.astype(jnp.int32), nsub, first, n_tiles * MOE_TILE


def kernel(x_prompt, x_sample, norm1_g, w_in, conv_w, attn_norm_g, conv_norm_g, w_out, norm2_g,
           w_router, b_router, w_gate, b_gate, w_up, b_up, w_down, b_down, final_norm_g):
    assert norm1_g.shape[0] == 1, "one layer"
    w_in_b = w_in[0].astype(BF16)
    w_out_b = w_out[0].astype(BF16)
    wr = jnp.pad(w_router[0], ((0, 0), (0, LANES - N_EXPERTS)))
    wr_hi = wr.astype(BF16)
    wr_lo = (wr - wr_hi.astype(F32)).astype(BF16)
    br = jnp.pad(b_router[0], (0, LANES - N_EXPERTS), constant_values=-jnp.inf).reshape(1, LANES)
    g1 = norm1_g[0].reshape(1, D_MODEL)
    g2 = norm2_g[0].reshape(1, D_MODEL)
    ga = attn_norm_g[0].reshape(1, ATTN_WIDTH)
    gc = conv_norm_g[0].reshape(1, CONV_WIDTH)
    gf = final_norm_g.reshape(1, D_MODEL)

    x1s, h2s, idxs, gates = [], [], [], []
    for x in (x_prompt, x_sample):
        B, S, _ = x.shape
        x2d = x.reshape(B * S, D_MODEL)
        qkv, gates3 = _in_projection(x2d, S, g1, w_in_b, _rope_tables(S))
        attn = _band_attention(qkv, B, S, ga)
        x1, h2, idx, gate = _mixer_out(x2d, S, attn, gates3, conv_w[0], gc, w_out_b, g2, wr_hi, wr_lo, br)
        x1s.append(x1)
        h2s.append(h2)
        idxs.append(idx[:, :TOP_K])
        gates.append(gate)

    slot, notfull, te, tx, nsub, first, P = _route(jnp.concatenate(idxs, axis=0))
    slots = []
    t0 = 0
    for h2 in h2s:
        slots.append(slot[t0:t0 + h2.shape[0]])
        t0 += h2.shape[0]
    xs = _dispatch(h2s[0], h2s[1], slot, notfull, P)
    y = _expert_ffn(xs, te, tx, nsub, first, w_gate[0], b_gate[0], w_up[0], b_up[0], w_down[0], b_down[0])
    return tuple(_combine_final(x1, gate, sl, y, gf).reshape(x.shape)
                 for x, x1, gate, sl in zip((x_prompt, x_sample), x1s, gates, slots))
```

```python
import functools

import jax
import jax.numpy as jnp
from jax import lax
from jax.experimental import pallas as pl
from jax.experimental.pallas import tpu as pltpu

F32 = jnp.float32
BF16 = jnp.bfloat16

D_MODEL = 2048
HEAD_DIM = 128
N_HEADS = 8
ATTN_WIDTH = N_HEADS * HEAD_DIM
CONV_WIDTH = D_MODEL - ATTN_WIDTH
N_CONV_GROUPS = 8
QKV_WIDTH = 3 * ATTN_WIDTH
GATE_WIDTH = 3 * CONV_WIDTH
DILATED_PATTERNS = ((128, 1), (512, 4), (2048, 16))
ROPE_THETA = 500000.0
ROPE_DIM = HEAD_DIM // 4
ROPE_HALF = ROPE_DIM // 2
N_EXPERTS = 32
TOP_K = 4
D_FF = D_MODEL
SWIGLU_LIMIT = 7.0
SWIGLU_ALPHA = 1.702
EPS = 1e-5

LANES = 128
BF16_SUBLANES = 16
VMEM_LIMIT = 56 * 1024 * 1024

IN_TM = 512
IN_TN = 1024
ATTN_QB = 128
ATTN_UNROLL = 8
ATTN_FIN = 256
MIX_TM = 256
DISP_R = 256
MOE_SUB = 256
MOE_TILE = 512
MOE_NSUB = MOE_TILE // MOE_SUB
MOE_TF = 256
FIN_TM = 256


def _params(sem):
    return pltpu.CompilerParams(dimension_semantics=sem, vmem_limit_bytes=VMEM_LIMIT)


def _normed(x_ref, g_ref, h_scr):
    x = x_ref[...]
    ms = jnp.mean(x * x, axis=-1, keepdims=True)
    h_scr[...] = (x * lax.rsqrt(ms + EPS) * g_ref[...]).astype(BF16)


def _qkv_body(x_ref, g_ref, w_ref, c_ref, sa_ref, sb_ref, o_ref, h_scr):
    pl.when(pl.program_id(1) == 0)(functools.partial(_normed, x_ref, g_ref, h_scr))
    acc = jnp.dot(h_scr[...], w_ref[...], preferred_element_type=F32)
    c, sa, sb = c_ref[...], sa_ref[...], sb_ref[...]
    for h in range(N_HEADS):
        sl = slice(h * HEAD_DIM, (h + 1) * HEAD_DIM)
        a = acc[:, sl]
        o_ref[:, sl] = a * c + pltpu.roll(a, HEAD_DIM - ROPE_HALF, 1) * sa + pltpu.roll(a, ROPE_HALF, 1) * sb


def _gates_body(x_ref, g_ref, w_ref, o_ref, h_scr):
    pl.when(pl.program_id(1) == 0)(functools.partial(_normed, x_ref, g_ref, h_scr))
    o_ref[...] = jnp.dot(h_scr[...], w_ref[...], preferred_element_type=F32).astype(BF16)


def _rope_tables(S):
    inv_freq = ROPE_THETA ** (-jnp.arange(ROPE_HALF, dtype=F32) * 2.0 / ROPE_DIM)
    ang = jnp.arange(S, dtype=F32)[:, None] * inv_freq[None, :]
    cos, sin = jnp.cos(ang), jnp.sin(ang)
    pad = jnp.zeros((S, HEAD_DIM - ROPE_DIM), F32)
    zero = jnp.zeros((S, ROPE_HALF), F32)
    c = jnp.concatenate([cos, cos, pad + 1.0], axis=1)
    sa = jnp.concatenate([-sin, zero, pad], axis=1)
    sb = jnp.concatenate([zero, sin, pad], axis=1)
    scale = HEAD_DIM ** -0.5
    none = jnp.zeros_like(c)
    return (jnp.stack([c * scale, c, none + 1.0]), jnp.stack([sa * scale, sa, none]),
            jnp.stack([sb * scale, sb, none]))


def _in_projection(x2d, S, gain, w_bf16, tables):
    T = x2d.shape[0]
    tm = IN_TM
    pos_blocks = S // tm
    n_qkv = QKV_WIDTH // IN_TN
    x_spec = pl.BlockSpec((tm, D_MODEL), lambda i, j: (i, 0))
    g_spec = pl.BlockSpec((1, D_MODEL), lambda i, j: (0, 0))
    out_spec = pl.BlockSpec((tm, IN_TN), lambda i, j: (i, j))
    tab_spec = pl.BlockSpec((None, tm, HEAD_DIM), lambda i, j: (j, i % pos_blocks, 0))
    common = dict(scratch_shapes=[pltpu.VMEM((tm, D_MODEL), BF16)],
                  compiler_params=_params(("arbitrary", "arbitrary")))
    qkv = pl.pallas_call(
        _qkv_body,
        grid=(T // tm, n_qkv),
        in_specs=[x_spec, g_spec, pl.BlockSpec((D_MODEL, IN_TN), lambda i, j: (0, j)), tab_spec, tab_spec, tab_spec],
        out_specs=out_spec,
        out_shape=jax.ShapeDtypeStruct((T, QKV_WIDTH), F32),
        name="qkv_projection", **common,
    )(x2d, gain, w_bf16, *tables)
    gates = pl.pallas_call(
        _gates_body,
        grid=(T // tm, GATE_WIDTH // IN_TN),
        in_specs=[x_spec, g_spec, pl.BlockSpec((D_MODEL, IN_TN), lambda i, j: (0, n_qkv + j))],
        out_specs=out_spec,
        out_shape=jax.ShapeDtypeStruct((T, GATE_WIDTH), BF16),
        name="gate_projection", **common,
    )(x2d, gain, w_bf16)
    return qkv, gates


def _attn_configs(S):
    cfgs, tables = [], []
    for window, d in DILATED_PATTERNS:
        L = S // d
        qb = min(ATTN_QB, L)
        half = window // (2 * d)
        W = min(qb + 2 * half, L)
        assert L % qb == 0 and window % (2 * d) == 0
        assert (W == qb + 2 * half and L // qb >= 2) or (L == qb == W)
        if (qb, W, half) not in tables:
            tables.append((qb, W, half))
        cfgs.append((d, L, qb, W, half, tables.index((qb, W, half))))
    return cfgs, tables


def _mask_table(qb, W, half):
    rel = jnp.arange(W, dtype=jnp.int32)[None, :] - jnp.arange(qb, dtype=jnp.int32)[:, None]
    offs = jnp.array([half, 0, -half], jnp.int32)[:, None, None]
    u = rel[None] + offs
    return jnp.where((u >= 0) & (u <= 2 * half), 0.0, -1e30).astype(F32)


def _attn_body(*refs, S):
    cfgs, tables = _attn_configs(S)
    q_ref, k_ref, v_ref, gain_ref = refs[:4]
    mask_refs = refs[4:4 + len(tables)]
    o_ref, qc, kc, vc, og, lg, sc, mc, q32, k32, v32 = refs[4 + len(tables):]
    srcs, stages, f32s = (q_ref, k_ref, v_ref), (qc, kc, vc), (q32, k32, v32)
    d_prev = 1
    for g, (d, L, qb, W, half, tab) in enumerate(cfgs):
        n_blocks = L // qb
        mask_ref = mask_refs[tab]

        def rows_of_class(r, start, size, d=d):
            if d == 1:
                return pl.ds(start, size)
            return pl.ds(r + start * d, size, stride=d)

        from_prev = d_prev > 1 and d % d_prev == 0
        keep_f32 = any(c[0] > d and c[0] % d == 0 for c in cfgs) and d > 1
        step = d // d_prev if from_prev else d

        def stage(r, carry, d=d, L=L, qb=qb, from_prev=from_prev, keep_f32=keep_f32, step=step, d_prev=d_prev):
            dst = pl.ds(pl.multiple_of(r * L, qb), L)
            for src, st, f32 in zip(srcs, stages, f32s):
                if from_prev:
                    start = (r % d_prev) * (L * step) + r // d_prev
                    x = f32[pl.ds(start, L, stride=step), :]
                elif d == 1:
                    x = src[pl.ds(0, L), :]
                else:
                    x = src[pl.ds(r, L, stride=d), :]
                if keep_f32:
                    f32[dst, :] = x
                st[dst, :] = x.astype(BF16)
            return carry

        if d == 1:
            stage(0, 0)
        else:
            lax.fori_loop(0, d, stage, 0)
        d_prev = d

        def place(nb, L=L, qb=qb, half=half, W=W, n_blocks=n_blocks):
            r = nb // n_blocks
            q0 = pl.multiple_of((nb % n_blocks) * qb, qb)
            ws = jnp.clip(q0 - half, 0, L - W)
            case = jnp.where(q0 == 0, 0, jnp.where(q0 == L - qb, 2, 1))
            keys = pl.ds(pl.multiple_of(r * L + ws, BF16_SUBLANES), W)
            return r, q0, keys, case

        def scores(nb, c, g=g, L=L, qb=qb, W=W, mask_ref=mask_ref, place=place, rows_of_class=rows_of_class):
            r, q0, keys, case = place(nb)
            q = qc[pl.ds(pl.multiple_of(r * L + q0, qb), qb), :]
            s = lax.dot_general(q, kc[keys, :], (((1,), (1,)), ((), ())), preferred_element_type=F32)
            s = s + mask_ref[case]
            m = jnp.max(s, axis=-1, keepdims=True)
            sc[nb, 0:qb, 0:W] = s - m
            mc[pl.ds(pl.multiple_of(nb * qb, qb), qb), :] = jnp.broadcast_to(m, (qb, LANES))
            return c

        def values(nb, c, g=g, qb=qb, W=W, place=place, rows_of_class=rows_of_class):
            r, q0, keys, _ = place(nb)
            p = jnp.exp(sc[nb, 0:qb, 0:W])
            l = jnp.sum(p, axis=-1, keepdims=True)
            acc = jnp.dot(p.astype(BF16), vc[keys, :], preferred_element_type=F32)
            og[g, rows_of_class(r, q0, qb), :] = acc / l
            lg[g, rows_of_class(r, q0, qb), :] = mc[pl.ds(pl.multiple_of(nb * qb, qb), qb), :] + jnp.log(l)
            return c

        unroll = min(ATTN_UNROLL, S // qb)
        lax.fori_loop(0, S // qb, scores, 0, unroll=unroll)
        lax.fori_loop(0, S // qb, values, 0, unroll=unroll)

    def combine(c, carry):
        rows = pl.ds(pl.multiple_of(c * ATTN_FIN, ATTN_FIN), ATTN_FIN)
        l0, l1, l2 = lg[0, rows, :], lg[1, rows, :], lg[2, rows, :]
        mx = jnp.maximum(jnp.maximum(l0, l1), l2)
        e0, e1, e2 = jnp.exp(l0 - mx), jnp.exp(l1 - mx), jnp.exp(l2 - mx)
        a = (e0 * og[0, rows, :] + e1 * og[1, rows, :] + e2 * og[2, rows, :]) / (e0 + e1 + e2)
        ms = jnp.mean(a * a, axis=-1, keepdims=True)
        o_ref[rows, :] = (a * lax.rsqrt(ms + EPS) * gain_ref[...]).astype(BF16)
        return carry

    lax.fori_loop(0, S // ATTN_FIN, combine, 0)


def _band_attention(qkv, B, S, gain):
    cfgs, tables = _attn_configs(S)
    score_shape = (max(S // c[2] for c in cfgs), max(c[2] for c in cfgs), max(c[3] for c in cfgs))
    head = lambda t: pl.BlockSpec((S, HEAD_DIM), lambda b, h: (b, t * N_HEADS + h))
    masks = [_mask_table(*t) for t in tables]
    mask_specs = [pl.BlockSpec(m.shape, lambda b, h: (0, 0, 0)) for m in masks]
    return pl.pallas_call(
        functools.partial(_attn_body, S=S),
        grid=(B, N_HEADS),
        in_specs=[head(0), head(1), head(2), pl.BlockSpec((1, HEAD_DIM), lambda b, h: (0, h))] + mask_specs,
        out_specs=pl.BlockSpec((S, HEAD_DIM), lambda b, h: (b, h)),
        out_shape=jax.ShapeDtypeStruct((B * S, ATTN_WIDTH), BF16),
        scratch_shapes=[pltpu.VMEM((S, HEAD_DIM), BF16)] * 3
        + [pltpu.VMEM((len(DILATED_PATTERNS), S, HEAD_DIM), F32)] * 2
        + [pltpu.VMEM(score_shape, F32)] + [pltpu.VMEM((S, HEAD_DIM), F32)] * 4,
        compiler_params=_params(("arbitrary", "arbitrary")),
        name="band_attention",
    )(qkv, qkv, qkv, gain, *masks)


def _mix_body(attn_ref, cb_ref, cc_ref, cx_ref, ccp_ref, cxp_ref, ccn_ref, cxn_ref, x_ref, wout_ref,
              convw_ref, gc_ref, g2_ref, wrh_ref, wrl_ref, br_ref, tri_ref, base_ref,
              x1_ref, h2_ref, idx_ref, gate_ref, rank_ref, cnt_ref, conv_scr, cnt_scr, *, tm, S):
    i = pl.program_id(0)

    row = lax.broadcasted_iota(jnp.int32, (tm, LANES), 0)
    r0 = i * tm
    at_start = (r0 % S) == 0
    at_end = ((r0 + tm) % S) == 0
    last = BF16_SUBLANES - 1
    gw = CONV_WIDTH // N_CONV_GROUPS
    for c in range(N_CONV_GROUPS):
        sl = slice(c * gw, (c + 1) * gw)
        u = cc_ref[:, sl].astype(F32) * cx_ref[:, sl].astype(F32)
        up = ccp_ref[last:last + 1, sl].astype(F32) * cxp_ref[last:last + 1, sl].astype(F32)
        un = ccn_ref[0:1, sl].astype(F32) * cxn_ref[0:1, sl].astype(F32)
        up = jnp.where(at_start, 0.0, up)
        un = jnp.where(at_end, 0.0, un)
        u_prev = jnp.where(row == 0, up, pltpu.roll(u, 1, 0))
        u_next = jnp.where(row == tm - 1, un, pltpu.roll(u, tm - 1, 0))
        y = convw_ref[0:1, sl] * u_prev + convw_ref[1:2, sl] * u + convw_ref[2:3, sl] * u_next
        cv = cb_ref[:, sl].astype(F32) * y
        ms = jnp.mean(cv * cv, axis=-1, keepdims=True)
        conv_scr[:, sl] = (cv * lax.rsqrt(ms + EPS) * gc_ref[:, sl]).astype(BF16)

    mix = (jnp.dot(attn_ref[...], wout_ref[0:ATTN_WIDTH, :], preferred_element_type=F32)
           + jnp.dot(conv_scr[...], wout_ref[ATTN_WIDTH:D_MODEL, :], preferred_element_type=F32))
    x1 = x_ref[...] + mix
    x1_ref[...] = x1
    ms = jnp.mean(x1 * x1, axis=-1, keepdims=True)
    h2 = x1 * lax.rsqrt(ms + EPS) * g2_ref[...]
    hi = h2.astype(BF16)
    hi_f = hi.astype(F32)
    h2_ref[...] = hi_f

    lo = (h2 - hi_f).astype(BF16)
    logits = (jnp.dot(hi, wrh_ref[...], preferred_element_type=F32)
              + jnp.dot(lo, wrh_ref[...], preferred_element_type=F32)
              + jnp.dot(hi, wrl_ref[...], preferred_element_type=F32)) + br_ref[...]

    lane = lax.broadcasted_iota(jnp.int32, (tm, LANES), 1)
    lane_f = lane.astype(F32)
    vals, ids = [], []
    for _ in range(TOP_K):
        m = jnp.max(logits, axis=-1, keepdims=True)
        ix = jnp.min(jnp.where(logits == m, lane_f, float(LANES)), axis=-1, keepdims=True)
        vals.append(m)
        ids.append(ix)
        logits = jnp.where(lane_f == ix, -jnp.inf, logits)
    es = [jnp.exp(v - vals[0]) for v in vals]
    den = es[0] + es[1] + es[2] + es[3]
    idx_tile = jnp.zeros((tm, LANES), F32)
    gate_tile = jnp.zeros((tm, LANES), F32)
    for k in range(TOP_K):
        idx_tile = jnp.where(lane == k, ids[k], idx_tile)
        gate_tile = jnp.where(lane == k, es[k] / den, gate_tile)
    idx_ref[...] = idx_tile.astype(jnp.int32)
    gate_ref[...] = gate_tile

    @pl.when(i == 0)
    def _():
        cnt_scr[...] = base_ref[...]

    chosen = [lane_f == ids[k] for k in range(TOP_K)]
    picks = sum(c.astype(F32) for c in chosen)
    before = cnt_scr[...] + jnp.dot(tri_ref[...], picks.astype(BF16), preferred_element_type=F32)
    rank_tile = jnp.zeros((tm, LANES), F32)
    for k in range(TOP_K):
        rk = jnp.sum(jnp.where(chosen[k], before, 0.0), axis=-1, keepdims=True)
        rank_tile = jnp.where(lane == k, rk, rank_tile)
    rank_ref[...] = rank_tile.astype(jnp.int32)
    cnt_scr[...] += jnp.sum(picks, axis=0, keepdims=True)
    cnt_ref[...] = cnt_scr[...]


def _mixer_out(x2d, S, attn, gates3, conv_w, gc, w_out, g2, wr_hi, wr_lo, br, base_counts):
    T = x2d.shape[0]
    tm = MIX_TM
    tri = (jnp.arange(tm)[:, None] > jnp.arange(tm)[None, :]).astype(BF16)
    hr = BF16_SUBLANES
    n_halo = T // hr
    row_tile = lambda w: pl.BlockSpec((tm, w), lambda i: (i, 0))
    gate_tile = lambda c: pl.BlockSpec((tm, CONV_WIDTH), lambda i: (i, c))
    prev_halo = lambda c: pl.BlockSpec((hr, CONV_WIDTH), lambda i: (jnp.maximum(i * (tm // hr) - 1, 0), c))
    next_halo = lambda c: pl.BlockSpec(
        (hr, CONV_WIDTH), lambda i: (jnp.minimum((i + 1) * (tm // hr), n_halo - 1), c))
    const = lambda shape: pl.BlockSpec(shape, lambda i: (0, 0))
    cb_col, cc_col, cx_col = 0, 1, 2
    return pl.pallas_call(
        functools.partial(_mix_body, tm=tm, S=S),
        grid=(T // tm,),
        in_specs=[
            row_tile(ATTN_WIDTH),
            gate_tile(cb_col), gate_tile(cc_col), gate_tile(cx_col),
            prev_halo(cc_col), prev_halo(cx_col), next_halo(cc_col), next_halo(cx_col),
            row_tile(D_MODEL),
            const((D_MODEL, D_MODEL)),
            const((3, CONV_WIDTH)), const((1, CONV_WIDTH)), const((1, D_MODEL)),
            const((D_MODEL, LANES)), const((D_MODEL, LANES)), const((1, LANES)),
            const((tm, tm)), const((1, LANES)),
        ],
        out_specs=[row_tile(D_MODEL), row_tile(D_MODEL), row_tile(LANES), row_tile(LANES), row_tile(LANES),
                   const((1, LANES))],
        out_shape=[
            jax.ShapeDtypeStruct((T, D_MODEL), F32),
            jax.ShapeDtypeStruct((T, D_MODEL), F32),
            jax.ShapeDtypeStruct((T, LANES), jnp.int32),
            jax.ShapeDtypeStruct((T, LANES), F32),
            jax.ShapeDtypeStruct((T, LANES), jnp.int32),
            jax.ShapeDtypeStruct((1, LANES), F32),
        ],
        scratch_shapes=[pltpu.VMEM((tm, CONV_WIDTH), BF16), pltpu.VMEM((1, LANES), F32)],
        compiler_params=_params(("arbitrary",)),
        name="mixer_out_router",
    )(attn, gates3, gates3, gates3, gates3, gates3, gates3, gates3, x2d, w_out, conv_w,
      gc, g2, wr_hi, wr_lo, br, tri, base_counts)


def _dispatch_body(nf_ref, slot_ref, ha_ref, hb_ref, xs_hbm, zero_scr, zsem, sem, *, steps_a):
    i = pl.program_id(0)
    n_sub = xs_hbm.shape[0] // MOE_SUB

    def zero_copy(sb):
        return pltpu.make_async_copy(zero_scr, xs_hbm.at[pl.ds(sb * MOE_SUB, MOE_SUB), :], zsem)

    @pl.when(i == 0)
    def _():
        zero_scr[...] = jnp.zeros_like(zero_scr)

        def start(sb, cnt):
            pl.when(nf_ref[sb] != 0)(lambda: zero_copy(sb).start())
            return cnt + nf_ref[sb]

        cnt = lax.fori_loop(0, n_sub, start, 0)

        def wait(_, c):
            zero_copy(0).wait()
            return c

        lax.fori_loop(0, cnt, wait, 0)

    def scatter(h_ref):
        for j in range(DISP_R):
            for k in range(TOP_K):
                s = slot_ref[0, j * TOP_K + k]
                pltpu.make_async_copy(h_ref.at[pl.ds(j, 1), :], xs_hbm.at[pl.ds(s, 1), :], sem).start(priority=k % 2)

    pl.when(i < steps_a)(lambda: scatter(ha_ref))
    pl.when(i >= steps_a)(lambda: scatter(hb_ref))
    n_rows = DISP_R * TOP_K
    pltpu.make_async_copy(xs_hbm.at[pl.ds(0, n_rows), :], xs_hbm.at[pl.ds(0, n_rows), :], sem).wait()


def _dispatch(h2_a, h2_b, slot, notfull, P):
    steps_a = h2_a.shape[0] // DISP_R
    steps_b = h2_b.shape[0] // DISP_R
    n_steps = steps_a + steps_b
    grid_spec = pltpu.PrefetchScalarGridSpec(
        num_scalar_prefetch=1,
        grid=(n_steps,),
        in_specs=[
            pl.BlockSpec((None, 1, DISP_R * TOP_K), lambda i, nf: (i, 0, 0), memory_space=pltpu.SMEM),
            pl.BlockSpec((DISP_R, D_MODEL), lambda i, nf: (jnp.minimum(i, steps_a - 1), 0)),
            pl.BlockSpec((DISP_R, D_MODEL), lambda i, nf: (jnp.maximum(i - steps_a, 0), 0)),
        ],
        out_specs=pl.BlockSpec(memory_space=pl.ANY),
        scratch_shapes=[pltpu.VMEM((MOE_SUB, D_MODEL), F32), pltpu.SemaphoreType.DMA(()),
                        pltpu.SemaphoreType.DMA(())],
    )
    return pl.pallas_call(
        functools.partial(_dispatch_body, steps_a=steps_a),
        grid_spec=grid_spec,
        out_shape=jax.ShapeDtypeStruct((P, D_MODEL), F32),
        compiler_params=_params(("arbitrary",)),
        name="dispatch",
    )(notfull, slot.reshape(n_steps, 1, DISP_R * TOP_K), h2_a, h2_b)


def _expert_body(te_ref, tx_ref, ns_ref, first_ref, x_ref, bg_ref, bu_ref, bd_ref, wg_hbm, wu_hbm, wd_hbm,
                 y_ref, xb, wg_res, wu_res, wd_res, stg_g, stg_u, stg_d, sems):
    i = pl.program_id(0)
    e = te_ref[i]
    nsub = ns_ref[i]
    first = first_ref[i]
    nf = D_FF // MOE_TF
    kh = D_MODEL // 2
    fh = MOE_TF // 2

    for s in range(MOE_NSUB):
        rows = slice(s * MOE_SUB, (s + 1) * MOE_SUB)
        y_ref[rows, :] = jnp.where(s < nsub, jnp.broadcast_to(bd_ref[...], (MOE_SUB, D_MODEL)), 0.0)

    @pl.when(nsub > 0)
    def _():
        xb[...] = x_ref[...].astype(BF16)

    def sub_block(s, f):
        rows = slice(s * MOE_SUB, (s + 1) * MOE_SUB)
        x = xb[rows, :]
        g = jnp.dot(x, wg_res[f], preferred_element_type=F32) + bg_ref[f]
        u = jnp.dot(x, wu_res[f], preferred_element_type=F32) + bu_ref[f]
        g = jnp.minimum(g, SWIGLU_LIMIT)
        u = jnp.clip(u, -SWIGLU_LIMIT, SWIGLU_LIMIT)
        a = (u + 1.0) * (g * jax.nn.sigmoid(SWIGLU_ALPHA * g))
        y_ref[rows, :] += jnp.dot(a.astype(BF16), wd_res[f], preferred_element_type=F32)

    def half_copies(f, h):
        cols = pl.ds(f * MOE_TF, MOE_TF)
        drows = pl.ds(f * MOE_TF + h * fh, fh)
        return (pltpu.make_async_copy(wg_hbm.at[e, pl.ds(h * kh, kh), cols], stg_g.at[h], sems.at[0, h]),
                pltpu.make_async_copy(wu_hbm.at[e, pl.ds(h * kh, kh), cols], stg_u.at[h], sems.at[1, h]),
                pltpu.make_async_copy(wd_hbm.at[e, drows, :], stg_d.at[h], sems.at[2, h]))

    def load_and_compute(full_tile):
        for h in range(2):
            for c in half_copies(0, h):
                c.start()

        def body(f, carry):
            for h in range(2):
                for c in half_copies(f, h):
                    c.wait()
                wg_res[f, h * kh:(h + 1) * kh, :] = stg_g[h].astype(BF16)
                wu_res[f, h * kh:(h + 1) * kh, :] = stg_u[h].astype(BF16)
                wd_res[f, h * fh:(h + 1) * fh, :] = stg_d[h].astype(BF16)

                @pl.when(f + 1 < nf)
                def _():
                    for c in half_copies(f + 1, h):
                        c.start()
            for s in range(MOE_NSUB):
                if full_tile:
                    sub_block(s, f)
                else:
                    pl.when(s < nsub)(functools.partial(sub_block, s, f))
            return carry

        lax.fori_loop(0, nf, body, 0)

    pl.when((first == 1) & (nsub == MOE_NSUB))(functools.partial(load_and_compute, True))
    pl.when((first == 1) & (nsub < MOE_NSUB))(functools.partial(load_and_compute, False))

    @pl.when((first == 0) & (nsub == MOE_NSUB))
    def _():
        def full(f, carry):
            for s in range(MOE_NSUB):
                sub_block(s, f)
            return carry
        lax.fori_loop(0, nf, full, 0, unroll=4)

    @pl.when((first == 0) & (nsub > 0) & (nsub < MOE_NSUB))
    def _():
        def part(f, carry):
            for s in range(MOE_NSUB - 1):
                pl.when(s < nsub)(functools.partial(sub_block, s, f))
            return carry
        lax.fori_loop(0, nf, part, 0)


def _expert_ffn(xs, tile_expert, tile_xblk, tile_nsub, tile_first, w_gate, b_gate, w_up, b_up, w_down, b_down):
    P = xs.shape[0]
    n_tiles = P // MOE_TILE
    nf = D_FF // MOE_TF
    per_expert = lambda shape: pl.BlockSpec(
        (None,) + shape, lambda i, te, tx, ns, fi: (te[i],) + (0,) * len(shape))
    grid_spec = pltpu.PrefetchScalarGridSpec(
        num_scalar_prefetch=4,
        grid=(n_tiles,),
        in_specs=[
            pl.BlockSpec((MOE_TILE, D_MODEL), lambda i, te, tx, ns, fi: (tx[i], 0)),
            per_expert((nf, 1, MOE_TF)), per_expert((nf, 1, MOE_TF)), per_expert((1, D_MODEL)),
            pl.BlockSpec(memory_space=pl.ANY), pl.BlockSpec(memory_space=pl.ANY), pl.BlockSpec(memory_space=pl.ANY),
        ],
        out_specs=pl.BlockSpec((MOE_TILE, D_MODEL), lambda i, te, tx, ns, fi: (i, 0)),
        scratch_shapes=[
            pltpu.VMEM((MOE_TILE, D_MODEL), BF16),
            pltpu.VMEM((nf, D_MODEL, MOE_TF), BF16), pltpu.VMEM((nf, D_MODEL, MOE_TF), BF16),
            pltpu.VMEM((nf, MOE_TF, D_MODEL), BF16),
            pltpu.VMEM((2, D_MODEL // 2, MOE_TF), F32), pltpu.VMEM((2, D_MODEL // 2, MOE_TF), F32),
            pltpu.VMEM((2, MOE_TF // 2, D_MODEL), F32),
            pltpu.SemaphoreType.DMA((3, 2)),
        ],
    )
    return pl.pallas_call(
        _expert_body,
        grid_spec=grid_spec,
        out_shape=jax.ShapeDtypeStruct((P, D_MODEL), F32),
        compiler_params=_params(("arbitrary",)),
        name="expert_ffn",
    )(tile_expert, tile_xblk, tile_nsub, tile_first, xs,
      b_gate.reshape(N_EXPERTS, nf, 1, MOE_TF), b_up.reshape(N_EXPERTS, nf, 1, MOE_TF),
      b_down.reshape(N_EXPERTS, 1, D_MODEL), w_gate, w_up, w_down)


def _final_body(slot_ref, slot_next_ref, x1_ref, gate_ref, gf_ref, y_hbm, o_ref, ybuf, sems, *, n):
    i = pl.program_id(0)
    tm = FIN_TM

    def gather(s_ref, b):
        for j in range(tm):
            for k in range(TOP_K):
                s = s_ref[0, j * TOP_K + k]
                pltpu.make_async_copy(y_hbm.at[pl.ds(s, 1), :], ybuf.at[b, k, pl.ds(j, 1), :],
                                      sems.at[b]).start(priority=k % 2)

    b = i % 2
    pl.when(i == 0)(lambda: gather(slot_ref, 0))
    pl.when((i + 1 < n) & (b == 0))(lambda: gather(slot_next_ref, 1))
    pl.when((i + 1 < n) & (b == 1))(lambda: gather(slot_next_ref, 0))
    pltpu.make_async_copy(ybuf.at[b], ybuf.at[b], sems.at[b]).wait()

    gates = gate_ref[...]
    acc = x1_ref[...]
    for k in range(TOP_K):
        acc = acc + gates[:, k:k + 1] * ybuf[b, k]
    ms = jnp.mean(acc * acc, axis=-1, keepdims=True)
    o_ref[...] = acc * lax.rsqrt(ms + EPS) * gf_ref[...]


def _combine_final(x1, gates, slot, y, gf):
    T = x1.shape[0]
    tm = FIN_TM
    n_steps = T // tm
    row_tile = lambda w: pl.BlockSpec((tm, w), lambda i: (i, 0))
    slot3 = slot.reshape(n_steps, 1, tm * TOP_K)
    slot_spec = lambda step: pl.BlockSpec((None, 1, tm * TOP_K), step, memory_space=pltpu.SMEM)
    return pl.pallas_call(
        functools.partial(_final_body, n=n_steps),
        grid=(n_steps,),
        in_specs=[
            slot_spec(lambda i: (i, 0, 0)),
            slot_spec(lambda i: (jnp.minimum(i + 1, n_steps - 1), 0, 0)),
            row_tile(D_MODEL), row_tile(LANES),
            pl.BlockSpec((1, D_MODEL), lambda i: (0, 0)),
            pl.BlockSpec(memory_space=pl.ANY),
        ],
        out_specs=row_tile(D_MODEL),
        out_shape=jax.ShapeDtypeStruct((T, D_MODEL), F32),
        scratch_shapes=[pltpu.VMEM((2, TOP_K, tm, D_MODEL), F32), pltpu.SemaphoreType.DMA((2,))],
        compiler_params=_params(("arbitrary",)),
        name="combine_final_norm",
    )(slot3, slot3, x1, gates, gf, y)


def _route(idx, rank, counts):
    n = idx.shape[0] * TOP_K
    n_tiles = n // MOE_TILE + N_EXPERTS
    tiles_per_e = (counts + MOE_TILE - 1) // MOE_TILE
    tile_end = jnp.cumsum(tiles_per_e)
    tile_start = tile_end - tiles_per_e
    experts = jnp.arange(N_EXPERTS, dtype=jnp.int32)
    region = jnp.sum(jnp.where(idx[..., None] == experts, tile_start * MOE_TILE, 0), axis=-1)
    slot = region + rank

    t = jnp.arange(n_tiles, dtype=jnp.int32)
    used = t < tile_end[-1]
    last_used = jnp.maximum(tile_end[-1] - 1, 0)
    tt = jnp.where(used, t, last_used)
    te = jnp.minimum(jnp.sum(tt[:, None] >= tile_end[None, :], axis=1), N_EXPERTS - 1).astype(jnp.int32)
    rows = jnp.where(used, jnp.clip(counts[te] - (tt - tile_start[te]) * MOE_TILE, 0, MOE_TILE), 0)
    nsub = ((rows + MOE_SUB - 1) // MOE_SUB).astype(jnp.int32)
    sub_end = (jnp.arange(MOE_NSUB, dtype=jnp.int32) + 1) * MOE_SUB
    notfull = (rows[:, None] < sub_end[None, :]).astype(jnp.int32).reshape(-1)
    first = (used & (t == tile_start[te])).astype(jnp.int32)
    return slot, notfull, te, tt.astype(jnp.int32), nsub, first, n_tiles * MOE_TILE


def kernel(x_prompt, x_sample, norm1_g, w_in, conv_w, attn_norm_g, conv_norm_g, w_out, norm2_g,
           w_router, b_router, w_gate, b_gate, w_up, b_up, w_down, b_down, final_norm_g):
    assert norm1_g.shape[0] == 1, "one layer"
    w_in_b = w_in[0].astype(BF16)
    w_out_b = w_out[0].astype(BF16)
    wr = jnp.pad(w_router[0], ((0, 0), (0, LANES - N_EXPERTS)))
    wr_hi = wr.astype(BF16)
    wr_lo = (wr - wr_hi.astype(F32)).astype(BF16)
    br = jnp.pad(b_router[0], (0, LANES - N_EXPERTS), constant_values=-jnp.inf).reshape(1, LANES)
    g1 = norm1_g[0].reshape(1, D_MODEL)
    g2 = norm2_g[0].reshape(1, D_MODEL)
    ga = attn_norm_g[0].reshape(1, ATTN_WIDTH)
    gc = conv_norm_g[0].reshape(1, CONV_WIDTH)
    gf = final_norm_g.reshape(1, D_MODEL)

    x1s, h2s, idxs, ranks, gates = [], [], [], [], []
    counts = jnp.zeros((1, LANES), F32)
    for x in (x_prompt, x_sample):
        B, S, _ = x.shape
        x2d = x.reshape(B * S, D_MODEL)
        qkv, gates3 = _in_projection(x2d, S, g1, w_in_b, _rope_tables(S))
        attn = _band_attention(qkv, B, S, ga)
        x1, h2, idx, gate, rank, counts = _mixer_out(
            x2d, S, attn, gates3, conv_w[0], gc, w_out_b, g2, wr_hi, wr_lo, br, counts)
        x1s.append(x1)
        h2s.append(h2)
        idxs.append(idx[:, :TOP_K])
        ranks.append(rank[:, :TOP_K])
        gates.append(gate)

    slot, notfull, te, tx, nsub, first, P = _route(
        jnp.concatenate(idxs, axis=0), jnp.concatenate(ranks, axis=0), counts[0, :N_EXPERTS].astype(jnp.int32))
    slots = []
    t0 = 0
    for h2 in h2s:
        slots.append(slot[t0:t0 + h2.shape[0]])
        t0 += h2.shape[0]
    xs = _dispatch(h2s[0], h2s[1], slot, notfull, P)
    y = _expert_ffn(xs, te, tx, nsub, first, w_gate[0], b_gate[0], w_up[0], b_up[0], w_down[0], b_down[0])
    return tuple(_combine_final(x1, gate, sl, y, gf).reshape(x.shape)
                 for x, x1, gate, sl in zip((x_prompt, x_sample), x1s, gates, slots))
```

```python
import functools

import jax
import jax.numpy as jnp
from jax import lax
from jax.experimental import pallas as pl
from jax.experimental.pallas import tpu as pltpu

F32 = jnp.float32
BF16 = jnp.bfloat16

D_MODEL = 2048
HEAD_DIM = 128
N_HEADS = 8
ATTN_WIDTH = N_HEADS * HEAD_DIM
CONV_WIDTH = D_MODEL - ATTN_WIDTH
N_CONV_GROUPS = 8
QKV_WIDTH = 3 * ATTN_WIDTH
GATE_WIDTH = 3 * CONV_WIDTH
DILATED_PATTERNS = ((128, 1), (512, 4), (2048, 16))
ROPE_THETA = 500000.0
ROPE_DIM = HEAD_DIM // 4
ROPE_HALF = ROPE_DIM // 2
N_EXPERTS = 32
TOP_K = 4
D_FF = D_MODEL
SWIGLU_LIMIT = 7.0
SWIGLU_ALPHA = 1.702
EPS = 1e-5

LANES = 128
BF16_SUBLANES = 16
VMEM_LIMIT = 56 * 1024 * 1024

IN_TM = 512
IN_TN = 1024
ATTN_QB = 128
ATTN_UNROLL = 8
ATTN_FIN = 256
MIX_TM = 256
DISP_R = 256
MOE_SUB = 256
MOE_TILE = 512
MOE_NSUB = MOE_TILE // MOE_SUB
MOE_TF = 256
FIN_TM = 256


def _params(sem):
    return pltpu.CompilerParams(dimension_semantics=sem, vmem_limit_bytes=VMEM_LIMIT)


def _normed(x_ref, g_ref, h_scr):
    x = x_ref[...]
    ms = jnp.mean(x * x, axis=-1, keepdims=True)
    h_scr[...] = (x * lax.rsqrt(ms + EPS) * g_ref[...]).astype(BF16)


def _qkv_body(x_ref, g_ref, w_ref, c_ref, sa_ref, sb_ref, o_ref, h_scr):
    pl.when(pl.program_id(1) == 0)(functools.partial(_normed, x_ref, g_ref, h_scr))
    acc = jnp.dot(h_scr[...], w_ref[...], preferred_element_type=F32)
    c, sa, sb = c_ref[...], sa_ref[...], sb_ref[...]
    for h in range(N_HEADS):
        sl = slice(h * HEAD_DIM, (h + 1) * HEAD_DIM)
        a = acc[:, sl]
        o_ref[:, sl] = a * c + pltpu.roll(a, HEAD_DIM - ROPE_HALF, 1) * sa + pltpu.roll(a, ROPE_HALF, 1) * sb


def _gates_body(x_ref, g_ref, w_ref, o_ref, h_scr):
    pl.when(pl.program_id(1) == 0)(functools.partial(_normed, x_ref, g_ref, h_scr))
    o_ref[...] = jnp.dot(h_scr[...], w_ref[...], preferred_element_type=F32).astype(BF16)


def _rope_tables(S):
    inv_freq = ROPE_THETA ** (-jnp.arange(ROPE_HALF, dtype=F32) * 2.0 / ROPE_DIM)
    ang = jnp.arange(S, dtype=F32)[:, None] * inv_freq[None, :]
    cos, sin = jnp.cos(ang), jnp.sin(ang)
    pad = jnp.zeros((S, HEAD_DIM - ROPE_DIM), F32)
    zero = jnp.zeros((S, ROPE_HALF), F32)
    c = jnp.concatenate([cos, cos, pad + 1.0], axis=1)
    sa = jnp.concatenate([-sin, zero, pad], axis=1)
    sb = jnp.concatenate([zero, sin, pad], axis=1)
    scale = HEAD_DIM ** -0.5
    none = jnp.zeros_like(c)
    return (jnp.stack([c * scale, c, none + 1.0]), jnp.stack([sa * scale, sa, none]),
            jnp.stack([sb * scale, sb, none]))


def _in_projection(x2d, S, gain, w_bf16, tables):
    T = x2d.shape[0]
    tm = IN_TM
    pos_blocks = S // tm
    n_qkv = QKV_WIDTH // IN_TN
    x_spec = pl.BlockSpec((tm, D_MODEL), lambda i, j: (i, 0))
    g_spec = pl.BlockSpec((1, D_MODEL), lambda i, j: (0, 0))
    out_spec = pl.BlockSpec((tm, IN_TN), lambda i, j: (i, j))
    tab_spec = pl.BlockSpec((None, tm, HEAD_DIM), lambda i, j: (j, i % pos_blocks, 0))
    common = dict(scratch_shapes=[pltpu.VMEM((tm, D_MODEL), BF16)],
                  compiler_params=_params(("arbitrary", "arbitrary")))
    qkv = pl.pallas_call(
        _qkv_body,
        grid=(T // tm, n_qkv),
        in_specs=[x_spec, g_spec, pl.BlockSpec((D_MODEL, IN_TN), lambda i, j: (0, j)), tab_spec, tab_spec, tab_spec],
        out_specs=out_spec,
        out_shape=jax.ShapeDtypeStruct((T, QKV_WIDTH), F32),
        name="qkv_projection", **common,
    )(x2d, gain, w_bf16, *tables)
    gates = pl.pallas_call(
        _gates_body,
        grid=(T // tm, GATE_WIDTH // IN_TN),
        in_specs=[x_spec, g_spec, pl.BlockSpec((D_MODEL, IN_TN), lambda i, j: (0, n_qkv + j))],
        out_specs=out_spec,
        out_shape=jax.ShapeDtypeStruct((T, GATE_WIDTH), BF16),
        name="gate_projection", **common,
    )(x2d, gain, w_bf16)
    return qkv, gates


def _attn_configs(S):
    cfgs, tables = [], []
    for window, d in DILATED_PATTERNS:
        L = S // d
        qb = min(ATTN_QB, L)
        half = window // (2 * d)
        W = min(qb + 2 * half, L)
        assert L % qb == 0 and window % (2 * d) == 0
        assert (W == qb + 2 * half and L // qb >= 2) or (L == qb == W)
        if (qb, W, half) not in tables:
            tables.append((qb, W, half))
        cfgs.append((d, L, qb, W, half, tables.index((qb, W, half))))
    return cfgs, tables


def _mask_table(qb, W, half):
    rel = jnp.arange(W, dtype=jnp.int32)[None, :] - jnp.arange(qb, dtype=jnp.int32)[:, None]
    offs = jnp.array([half, 0, -half], jnp.int32)[:, None, None]
    u = rel[None] + offs
    return jnp.where((u >= 0) & (u <= 2 * half), 0.0, -1e30).astype(F32)


def _attn_body(*refs, S):
    cfgs, tables = _attn_configs(S)
    q_ref, k_ref, v_ref, gain_ref = refs[:4]
    mask_refs = refs[4:4 + len(tables)]
    o_ref, qc, kc, vc, og, lg, sc, mc, q32, k32, v32 = refs[4 + len(tables):]
    srcs, stages, f32s = (q_ref, k_ref, v_ref), (qc, kc, vc), (q32, k32, v32)
    d_prev = 1
    for g, (d, L, qb, W, half, tab) in enumerate(cfgs):
        n_blocks = L // qb
        mask_ref = mask_refs[tab]

        def rows_of_class(r, start, size, d=d):
            if d == 1:
                return pl.ds(start, size)
            return pl.ds(r + start * d, size, stride=d)

        from_prev = d_prev > 1 and d % d_prev == 0
        keep_f32 = any(c[0] > d and c[0] % d == 0 for c in cfgs) and d > 1
        step = d // d_prev if from_prev else d

        def stage(r, carry, d=d, L=L, qb=qb, from_prev=from_prev, keep_f32=keep_f32, step=step, d_prev=d_prev):
            dst = pl.ds(pl.multiple_of(r * L, qb), L)
            for src, st, f32 in zip(srcs, stages, f32s):
                if from_prev:
                    start = (r % d_prev) * (L * step) + r // d_prev
                    x = f32[pl.ds(start, L, stride=step), :]
                elif d == 1:
                    x = src[pl.ds(0, L), :]
                else:
                    x = src[pl.ds(r, L, stride=d), :]
                if keep_f32:
                    f32[dst, :] = x
                st[dst, :] = x.astype(BF16)
            return carry

        if d == 1:
            stage(0, 0)
        else:
            lax.fori_loop(0, d, stage, 0)
        d_prev = d

        def place(nb, L=L, qb=qb, half=half, W=W, n_blocks=n_blocks):
            r = nb // n_blocks
            q0 = pl.multiple_of((nb % n_blocks) * qb, qb)
            ws = jnp.clip(q0 - half, 0, L - W)
            case = jnp.where(q0 == 0, 0, jnp.where(q0 == L - qb, 2, 1))
            keys = pl.ds(pl.multiple_of(r * L + ws, BF16_SUBLANES), W)
            return r, q0, keys, case

        def scores(nb, c, g=g, L=L, qb=qb, W=W, mask_ref=mask_ref, place=place, rows_of_class=rows_of_class):
            r, q0, keys, case = place(nb)
            q = qc[pl.ds(pl.multiple_of(r * L + q0, qb), qb), :]
            s = lax.dot_general(q, kc[keys, :], (((1,), (1,)), ((), ())), preferred_element_type=F32)
            s = s + mask_ref[case]
            m = jnp.max(s, axis=-1, keepdims=True)
            sc[nb, 0:qb, 0:W] = s - m
            mc[pl.ds(pl.multiple_of(nb * qb, qb), qb), :] = jnp.broadcast_to(m, (qb, LANES))
            return c

        def values(nb, c, g=g, qb=qb, W=W, place=place, rows_of_class=rows_of_class):
            r, q0, keys, _ = place(nb)
            p = jnp.exp(sc[nb, 0:qb, 0:W])
            l = jnp.sum(p, axis=-1, keepdims=True)
            acc = jnp.dot(p.astype(BF16), vc[keys, :], preferred_element_type=F32)
            og[g, rows_of_class(r, q0, qb), :] = acc / l
            lg[g, rows_of_class(r, q0, qb), :] = mc[pl.ds(pl.multiple_of(nb * qb, qb), qb), :] + jnp.log(l)
            return c

        unroll = min(ATTN_UNROLL, S // qb)
        lax.fori_loop(0, S // qb, scores, 0, unroll=unroll)
        lax.fori_loop(0, S // qb, values, 0, unroll=unroll)

    def combine(c, carry):
        rows = pl.ds(pl.multiple_of(c * ATTN_FIN, ATTN_FIN), ATTN_FIN)
        l0, l1, l2 = lg[0, rows, :], lg[1, rows, :], lg[2, rows, :]
        mx = jnp.maximum(jnp.maximum(l0, l1), l2)
        e0, e1, e2 = jnp.exp(l0 - mx), jnp.exp(l1 - mx), jnp.exp(l2 - mx)
        a = (e0 * og[0, rows, :] + e1 * og[1, rows, :] + e2 * og[2, rows, :]) / (e0 + e1 + e2)
        ms = jnp.mean(a * a, axis=-1, keepdims=True)
        o_ref[rows, :] = (a * lax.rsqrt(ms + EPS) * gain_ref[...]).astype(BF16)
        return carry

    lax.fori_loop(0, S // ATTN_FIN, combine, 0)


def _band_attention(qkv, B, S, gain):
    cfgs, tables = _attn_configs(S)
    score_shape = (max(S // c[2] for c in cfgs), max(c[2] for c in cfgs), max(c[3] for c in cfgs))
    head = lambda t: pl.BlockSpec((S, HEAD_DIM), lambda b, h: (b, t * N_HEADS + h))
    masks = [_mask_table(*t) for t in tables]
    mask_specs = [pl.BlockSpec(m.shape, lambda b, h: (0, 0, 0)) for m in masks]
    return pl.pallas_call(
        functools.partial(_attn_body, S=S),
        grid=(B, N_HEADS),
        in_specs=[head(0), head(1), head(2), pl.BlockSpec((1, HEAD_DIM), lambda b, h: (0, h))] + mask_specs,
        out_specs=pl.BlockSpec((S, HEAD_DIM), lambda b, h: (b, h)),
        out_shape=jax.ShapeDtypeStruct((B * S, ATTN_WIDTH), BF16),
        scratch_shapes=[pltpu.VMEM((S, HEAD_DIM), BF16)] * 3
        + [pltpu.VMEM((len(DILATED_PATTERNS), S, HEAD_DIM), F32)] * 2
        + [pltpu.VMEM(score_shape, F32)] + [pltpu.VMEM((S, HEAD_DIM), F32)] * 4,
        compiler_params=_params(("arbitrary", "arbitrary")),
        name="band_attention",
    )(qkv, qkv, qkv, gain, *masks)


def _mix_body(attn_ref, cb_ref, cc_ref, cx_ref, ccp_ref, cxp_ref, ccn_ref, cxn_ref, x_ref, wout_ref,
              convw_ref, gc_ref, g2_ref, wrh_ref, wrb_ref, br_ref, tri_ref, base_ref,
              x1_ref, h2_ref, idx_ref, gate_ref, rank_ref, cnt_ref, conv_scr, cnt_scr, *, tm, S, n):
    s = pl.program_id(0)
    live = s >= 1
    buf_a = s % 2
    buf_b = 1 - buf_a

    @pl.when(s == 0)
    def _():
        conv_scr[1] = jnp.zeros((tm, CONV_WIDTH), BF16)
        cnt_scr[...] = base_ref[...]

    mix = (jnp.dot(attn_ref[...], wout_ref[0:ATTN_WIDTH, :], preferred_element_type=F32)
           + jnp.dot(conv_scr[buf_b], wout_ref[ATTN_WIDTH:D_MODEL, :], preferred_element_type=F32))
    x1 = x_ref[...] + mix
    x1_ref[...] = x1
    ms = jnp.mean(x1 * x1, axis=-1, keepdims=True)
    h2 = x1 * lax.rsqrt(ms + EPS) * g2_ref[...]
    hi = h2.astype(BF16)
    hi_f = hi.astype(F32)
    h2_ref[...] = hi_f

    lo = (h2 - hi_f).astype(BF16)
    hi_both = jnp.dot(hi, wrb_ref[...], preferred_element_type=F32)
    logits = (hi_both[:, :LANES] + hi_both[:, LANES:]
              + jnp.dot(lo, wrh_ref[...], preferred_element_type=F32)) + br_ref[...]

    lane = lax.broadcasted_iota(jnp.int32, (tm, LANES), 1)
    lane_f = lane.astype(F32)
    vals, ids = [], []
    for _ in range(TOP_K):
        m = jnp.max(logits, axis=-1, keepdims=True)
        ix = jnp.min(jnp.where(logits == m, lane_f, float(LANES)), axis=-1, keepdims=True)
        vals.append(m)
        ids.append(ix)
        logits = jnp.where(lane_f == ix, -jnp.inf, logits)
    es = [jnp.exp(v - vals[0]) for v in vals]
    den = es[0] + es[1] + es[2] + es[3]
    idx_tile = jnp.zeros((tm, LANES), F32)
    gate_tile = jnp.zeros((tm, LANES), F32)
    for k in range(TOP_K):
        idx_tile = jnp.where(lane == k, ids[k], idx_tile)
        gate_tile = jnp.where(lane == k, es[k] / den, gate_tile)
    idx_ref[...] = idx_tile.astype(jnp.int32)
    gate_ref[...] = gate_tile

    chosen = [lane_f == ids[k] for k in range(TOP_K)]
    picks = sum(c.astype(F32) for c in chosen) * live.astype(F32)
    before = cnt_scr[...] + jnp.dot(tri_ref[...], picks.astype(BF16), preferred_element_type=F32)
    rank_tile = jnp.zeros((tm, LANES), F32)
    for k in range(TOP_K):
        rk = jnp.sum(jnp.where(chosen[k], before, 0.0), axis=-1, keepdims=True)
        rank_tile = jnp.where(lane == k, rk, rank_tile)
    rank_ref[...] = rank_tile.astype(jnp.int32)
    cnt_scr[...] += jnp.sum(picks, axis=0, keepdims=True)
    cnt_ref[...] = cnt_scr[...]

    row = lax.broadcasted_iota(jnp.int32, (tm, LANES), 0)
    r0 = jnp.minimum(s, n - 1) * tm
    at_start = (r0 % S) == 0
    at_end = ((r0 + tm) % S) == 0
    last = BF16_SUBLANES - 1
    gw = CONV_WIDTH // N_CONV_GROUPS
    for c in range(N_CONV_GROUPS):
        sl = slice(c * gw, (c + 1) * gw)
        u = cc_ref[:, sl].astype(F32) * cx_ref[:, sl].astype(F32)
        up = ccp_ref[last:last + 1, sl].astype(F32) * cxp_ref[last:last + 1, sl].astype(F32)
        un = ccn_ref[0:1, sl].astype(F32) * cxn_ref[0:1, sl].astype(F32)
        up = jnp.where(at_start, 0.0, up)
        un = jnp.where(at_end, 0.0, un)
        u_prev = jnp.where(row == 0, up, pltpu.roll(u, 1, 0))
        u_next = jnp.where(row == tm - 1, un, pltpu.roll(u, tm - 1, 0))
        y = convw_ref[0:1, sl] * u_prev + convw_ref[1:2, sl] * u + convw_ref[2:3, sl] * u_next
        cv = cb_ref[:, sl].astype(F32) * y
        ms = jnp.mean(cv * cv, axis=-1, keepdims=True)
        conv_scr[buf_a, :, sl] = (cv * lax.rsqrt(ms + EPS) * gc_ref[:, sl]).astype(BF16)


def _mixer_out(x2d, S, attn, gates3, conv_w, gc, w_out, g2, wr_hi, wr_both, br, base_counts):
    T = x2d.shape[0]
    tm = MIX_TM
    tri = (jnp.arange(tm)[:, None] > jnp.arange(tm)[None, :]).astype(BF16)
    hr = BF16_SUBLANES
    n_halo = T // hr
    n = T // tm
    done = lambda s: jnp.maximum(s - 1, 0)
    conv = lambda s: jnp.minimum(s, n - 1)
    row_tile = lambda w: pl.BlockSpec((tm, w), lambda s: (done(s), 0))
    gate_tile = lambda c: pl.BlockSpec((tm, CONV_WIDTH), lambda s: (conv(s), c))
    prev_halo = lambda c: pl.BlockSpec(
        (hr, CONV_WIDTH), lambda s: (jnp.maximum(conv(s) * (tm // hr) - 1, 0), c))
    next_halo = lambda c: pl.BlockSpec(
        (hr, CONV_WIDTH), lambda s: (jnp.minimum((conv(s) + 1) * (tm // hr), n_halo - 1), c))
    const = lambda shape: pl.BlockSpec(shape, lambda s: (0, 0))
    cb_col, cc_col, cx_col = 0, 1, 2
    return pl.pallas_call(
        functools.partial(_mix_body, tm=tm, S=S, n=n),
        grid=(n + 1,),
        in_specs=[
            row_tile(ATTN_WIDTH),
            gate_tile(cb_col), gate_tile(cc_col), gate_tile(cx_col),
            prev_halo(cc_col), prev_halo(cx_col), next_halo(cc_col), next_halo(cx_col),
            row_tile(D_MODEL),
            const((D_MODEL, D_MODEL)),
            const((3, CONV_WIDTH)), const((1, CONV_WIDTH)), const((1, D_MODEL)),
            const((D_MODEL, LANES)), const((D_MODEL, 2 * LANES)), const((1, LANES)),
            const((tm, tm)), const((1, LANES)),
        ],
        out_specs=[row_tile(D_MODEL), row_tile(D_MODEL), row_tile(LANES), row_tile(LANES), row_tile(LANES),
                   const((1, LANES))],
        out_shape=[
            jax.ShapeDtypeStruct((T, D_MODEL), F32),
            jax.ShapeDtypeStruct((T, D_MODEL), F32),
            jax.ShapeDtypeStruct((T, LANES), jnp.int32),
            jax.ShapeDtypeStruct((T, LANES), F32),
            jax.ShapeDtypeStruct((T, LANES), jnp.int32),
            jax.ShapeDtypeStruct((1, LANES), F32),
        ],
        scratch_shapes=[pltpu.VMEM((2, tm, CONV_WIDTH), BF16), pltpu.VMEM((1, LANES), F32)],
        compiler_params=_params(("arbitrary",)),
        name="mixer_out_router",
    )(attn, gates3, gates3, gates3, gates3, gates3, gates3, gates3, x2d, w_out, conv_w,
      gc, g2, wr_hi, wr_both, br, tri, base_counts)


def _dispatch_body(nf_ref, slot_ref, ha_ref, hb_ref, xs_hbm, zero_scr, zsem, sem, *, steps_a):
    i = pl.program_id(0)
    n_sub = xs_hbm.shape[0] // MOE_SUB

    def zero_copy(sb):
        return pltpu.make_async_copy(zero_scr, xs_hbm.at[pl.ds(sb * MOE_SUB, MOE_SUB), :], zsem)

    @pl.when(i == 0)
    def _():
        zero_scr[...] = jnp.zeros_like(zero_scr)

        def start(sb, cnt):
            pl.when(nf_ref[sb] != 0)(lambda: zero_copy(sb).start())
            return cnt + nf_ref[sb]

        cnt = lax.fori_loop(0, n_sub, start, 0)

        def wait(_, c):
            zero_copy(0).wait()
            return c

        lax.fori_loop(0, cnt, wait, 0)

    def scatter(h_ref):
        for j in range(DISP_R):
            for k in range(TOP_K):
                s = slot_ref[0, j * TOP_K + k]
                pltpu.make_async_copy(h_ref.at[pl.ds(j, 1), :], xs_hbm.at[pl.ds(s, 1), :], sem).start(priority=k % 2)

    pl.when(i < steps_a)(lambda: scatter(ha_ref))
    pl.when(i >= steps_a)(lambda: scatter(hb_ref))
    n_rows = DISP_R * TOP_K
    pltpu.make_async_copy(xs_hbm.at[pl.ds(0, n_rows), :], xs_hbm.at[pl.ds(0, n_rows), :], sem).wait()


def _dispatch(h2_a, h2_b, slot, notfull, P):
    steps_a = h2_a.shape[0] // DISP_R
    steps_b = h2_b.shape[0] // DISP_R
    n_steps = steps_a + steps_b
    grid_spec = pltpu.PrefetchScalarGridSpec(
        num_scalar_prefetch=1,
        grid=(n_steps,),
        in_specs=[
            pl.BlockSpec((None, 1, DISP_R * TOP_K), lambda i, nf: (i, 0, 0), memory_space=pltpu.SMEM),
            pl.BlockSpec((DISP_R, D_MODEL), lambda i, nf: (jnp.minimum(i, steps_a - 1), 0)),
            pl.BlockSpec((DISP_R, D_MODEL), lambda i, nf: (jnp.maximum(i - steps_a, 0), 0)),
        ],
        out_specs=pl.BlockSpec(memory_space=pl.ANY),
        scratch_shapes=[pltpu.VMEM((MOE_SUB, D_MODEL), F32), pltpu.SemaphoreType.DMA(()),
                        pltpu.SemaphoreType.DMA(())],
    )
    return pl.pallas_call(
        functools.partial(_dispatch_body, steps_a=steps_a),
        grid_spec=grid_spec,
        out_shape=jax.ShapeDtypeStruct((P, D_MODEL), F32),
        compiler_params=_params(("arbitrary",)),
        name="dispatch",
    )(notfull, slot.reshape(n_steps, 1, DISP_R * TOP_K), h2_a, h2_b)


def _expert_body(te_ref, tx_ref, ns_ref, first_ref, x_ref, bg_ref, bu_ref, bd_ref, wg_hbm, wu_hbm, wd_hbm,
                 y_ref, xb, wg_res, wu_res, wd_res, stg_g, stg_u, stg_d, sems):
    i = pl.program_id(0)
    e = te_ref[i]
    nsub = ns_ref[i]
    first = first_ref[i]
    nf = D_FF // MOE_TF
    kh = D_MODEL // 2
    fh = MOE_TF // 2

    full_later_tile = (first == 0) & (nsub == MOE_NSUB)

    @pl.when(jnp.logical_not(full_later_tile))
    def _():
        for s in range(MOE_NSUB):
            rows = slice(s * MOE_SUB, (s + 1) * MOE_SUB)
            y_ref[rows, :] = jnp.where(s < nsub, jnp.broadcast_to(bd_ref[...], (MOE_SUB, D_MODEL)), 0.0)

    @pl.when(nsub > 0)
    def _():
        xb[...] = x_ref[...].astype(BF16)

    def sub_block(s, f, from_bias=False):
        rows = slice(s * MOE_SUB, (s + 1) * MOE_SUB)
        x = xb[rows, :]
        g = jnp.dot(x, wg_res[f], preferred_element_type=F32) + bg_ref[f]
        u = jnp.dot(x, wu_res[f], preferred_element_type=F32) + bu_ref[f]
        g = jnp.minimum(g, SWIGLU_LIMIT)
        u = jnp.clip(u, -SWIGLU_LIMIT, SWIGLU_LIMIT)
        a = (u + 1.0) * (g * jax.nn.sigmoid(SWIGLU_ALPHA * g))
        dn = jnp.dot(a.astype(BF16), wd_res[f], preferred_element_type=F32)
        if from_bias:
            y_ref[rows, :] = bd_ref[...] + dn
        else:
            y_ref[rows, :] += dn

    def half_copies(f, h):
        cols = pl.ds(f * MOE_TF, MOE_TF)
        drows = pl.ds(f * MOE_TF + h * fh, fh)
        return (pltpu.make_async_copy(wg_hbm.at[e, pl.ds(h * kh, kh), cols], stg_g.at[h], sems.at[0, h]),
                pltpu.make_async_copy(wu_hbm.at[e, pl.ds(h * kh, kh), cols], stg_u.at[h], sems.at[1, h]),
                pltpu.make_async_copy(wd_hbm.at[e, drows, :], stg_d.at[h], sems.at[2, h]))

    @pl.when(first == 1)
    def _():
        for h in range(2):
            for c in half_copies(0, h):
                c.start()

        def load_and_compute(f, carry):
            for h in range(2):
                for c in half_copies(f, h):
                    c.wait()
                wg_res[f, h * kh:(h + 1) * kh, :] = stg_g[h].astype(BF16)
                wu_res[f, h * kh:(h + 1) * kh, :] = stg_u[h].astype(BF16)
                wd_res[f, h * fh:(h + 1) * fh, :] = stg_d[h].astype(BF16)

                @pl.when(f + 1 < nf)
                def _():
                    for c in half_copies(f + 1, h):
                        c.start()
            for s in range(MOE_NSUB):
                pl.when(s < nsub)(functools.partial(sub_block, s, f))
            return carry

        lax.fori_loop(0, nf, load_and_compute, 0)

    @pl.when(full_later_tile)
    def _():
        for s in range(MOE_NSUB):
            sub_block(s, 0, from_bias=True)

        def full(f, carry):
            for s in range(MOE_NSUB):
                sub_block(s, f)
            return carry
        lax.fori_loop(1, nf, full, 0, unroll=4)

    @pl.when((first == 0) & (nsub > 0) & (nsub < MOE_NSUB))
    def _():
        def part(f, carry):
            for s in range(MOE_NSUB - 1):
                pl.when(s < nsub)(functools.partial(sub_block, s, f))
            return carry
        lax.fori_loop(0, nf, part, 0)


def _expert_ffn(xs, tile_expert, tile_xblk, tile_nsub, tile_first, w_gate, b_gate, w_up, b_up, w_down, b_down):
    P = xs.shape[0]
    n_tiles = P // MOE_TILE
    nf = D_FF // MOE_TF
    per_expert = lambda shape: pl.BlockSpec(
        (None,) + shape, lambda i, te, tx, ns, fi: (te[i],) + (0,) * len(shape))
    grid_spec = pltpu.PrefetchScalarGridSpec(
        num_scalar_prefetch=4,
        grid=(n_tiles,),
        in_specs=[
            pl.BlockSpec((MOE_TILE, D_MODEL), lambda i, te, tx, ns, fi: (tx[i], 0)),
            per_expert((nf, 1, MOE_TF)), per_expert((nf, 1, MOE_TF)), per_expert((1, D_MODEL)),
            pl.BlockSpec(memory_space=pl.ANY), pl.BlockSpec(memory_space=pl.ANY), pl.BlockSpec(memory_space=pl.ANY),
        ],
        out_specs=pl.BlockSpec((MOE_TILE, D_MODEL), lambda i, te, tx, ns, fi: (i, 0)),
        scratch_shapes=[
            pltpu.VMEM((MOE_TILE, D_MODEL), BF16),
            pltpu.VMEM((nf, D_MODEL, MOE_TF), BF16), pltpu.VMEM((nf, D_MODEL, MOE_TF), BF16),
            pltpu.VMEM((nf, MOE_TF, D_MODEL), BF16),
            pltpu.VMEM((2, D_MODEL // 2, MOE_TF), F32), pltpu.VMEM((2, D_MODEL // 2, MOE_TF), F32),
            pltpu.VMEM((2, MOE_TF // 2, D_MODEL), F32),
            pltpu.SemaphoreType.DMA((3, 2)),
        ],
    )
    return pl.pallas_call(
        _expert_body,
        grid_spec=grid_spec,
        out_shape=jax.ShapeDtypeStruct((P, D_MODEL), F32),
        compiler_params=_params(("arbitrary",)),
        name="expert_ffn",
    )(tile_expert, tile_xblk, tile_nsub, tile_first, xs,
      b_gate.reshape(N_EXPERTS, nf, 1, MOE_TF), b_up.reshape(N_EXPERTS, nf, 1, MOE_TF),
      b_down.reshape(N_EXPERTS, 1, D_MODEL), w_gate, w_up, w_down)


def _final_body(slot_ref, slot_next_ref, x1_ref, gate_ref, gf_ref, y_hbm, o_ref, ybuf, sems, *, n):
    i = pl.program_id(0)
    tm = FIN_TM

    def gather(s_ref, b):
        for j in range(tm):
            for k in range(TOP_K):
                s = s_ref[0, j * TOP_K + k]
                pltpu.make_async_copy(y_hbm.at[pl.ds(s, 1), :], ybuf.at[b, k, pl.ds(j, 1), :],
                                      sems.at[b]).start(priority=k % 2)

    b = i % 2
    pl.when(i == 0)(lambda: gather(slot_ref, 0))
    pl.when((i + 1 < n) & (b == 0))(lambda: gather(slot_next_ref, 1))
    pl.when((i + 1 < n) & (b == 1))(lambda: gather(slot_next_ref, 0))
    pltpu.make_async_copy(ybuf.at[b], ybuf.at[b], sems.at[b]).wait()

    gates = gate_ref[...]
    acc = x1_ref[...]
    for k in range(TOP_K):
        acc = acc + gates[:, k:k + 1] * ybuf[b, k]
    ms = jnp.mean(acc * acc, axis=-1, keepdims=True)
    o_ref[...] = acc * lax.rsqrt(ms + EPS) * gf_ref[...]


def _combine_final(x1, gates, slot, y, gf):
    T = x1.shape[0]
    tm = FIN_TM
    n_steps = T // tm
    row_tile = lambda w: pl.BlockSpec((tm, w), lambda i: (i, 0))
    slot3 = slot.reshape(n_steps, 1, tm * TOP_K)
    slot_spec = lambda step: pl.BlockSpec((None, 1, tm * TOP_K), step, memory_space=pltpu.SMEM)
    return pl.pallas_call(
        functools.partial(_final_body, n=n_steps),
        grid=(n_steps,),
        in_specs=[
            slot_spec(lambda i: (i, 0, 0)),
            slot_spec(lambda i: (jnp.minimum(i + 1, n_steps - 1), 0, 0)),
            row_tile(D_MODEL), row_tile(LANES),
            pl.BlockSpec((1, D_MODEL), lambda i: (0, 0)),
            pl.BlockSpec(memory_space=pl.ANY),
        ],
        out_specs=row_tile(D_MODEL),
        out_shape=jax.ShapeDtypeStruct((T, D_MODEL), F32),
        scratch_shapes=[pltpu.VMEM((2, TOP_K, tm, D_MODEL), F32), pltpu.SemaphoreType.DMA((2,))],
        compiler_params=_params(("arbitrary",)),
        name="combine_final_norm",
    )(slot3, slot3, x1, gates, gf, y)


def _route(idx, rank, counts):
    n = idx.shape[0] * TOP_K
    n_tiles = n // MOE_TILE + N_EXPERTS
    tiles_per_e = (counts + MOE_TILE - 1) // MOE_TILE
    tile_end = jnp.cumsum(tiles_per_e)
    tile_start = tile_end - tiles_per_e
    experts = jnp.arange(N_EXPERTS, dtype=jnp.int32)
    region = jnp.sum(jnp.where(idx[..., None] == experts, tile_start * MOE_TILE, 0), axis=-1)
    slot = region + rank

    t = jnp.arange(n_tiles, dtype=jnp.int32)
    used = t < tile_end[-1]
    last_used = jnp.maximum(tile_end[-1] - 1, 0)
    tt = jnp.where(used, t, last_used)
    te = jnp.minimum(jnp.sum(tt[:, None] >= tile_end[None, :], axis=1), N_EXPERTS - 1).astype(jnp.int32)
    rows = jnp.where(used, jnp.clip(counts[te] - (tt - tile_start[te]) * MOE_TILE, 0, MOE_TILE), 0)
    nsub = ((rows + MOE_SUB - 1) // MOE_SUB).astype(jnp.int32)
    sub_end = (jnp.arange(MOE_NSUB, dtype=jnp.int32) + 1) * MOE_SUB
    notfull = (rows[:, None] < sub_end[None, :]).astype(jnp.int32).reshape(-1)
    first = (used & (t == tile_start[te])).astype(jnp.int32)
    return slot, notfull, te, tt.astype(jnp.int32), nsub, first, n_tiles * MOE_TILE


def kernel(x_prompt, x_sample, norm1_g, w_in, conv_w, attn_norm_g, conv_norm_g, w_out, norm2_g,
           w_router, b_router, w_gate, b_gate, w_up, b_up, w_down, b_down, final_norm_g):
    assert norm1_g.shape[0] == 1, "one layer"
    w_in_b = w_in[0].astype(BF16)
    w_out_b = w_out[0].astype(BF16)
    wr = jnp.pad(w_router[0], ((0, 0), (0, LANES - N_EXPERTS)))
    wr_hi = wr.astype(BF16)
    wr_both = jnp.concatenate([wr_hi, (wr - wr_hi.astype(F32)).astype(BF16)], axis=1)
    br = jnp.pad(b_router[0], (0, LANES - N_EXPERTS), constant_values=-jnp.inf).reshape(1, LANES)
    g1 = norm1_g[0].reshape(1, D_MODEL)
    g2 = norm2_g[0].reshape(1, D_MODEL)
    ga = attn_norm_g[0].reshape(1, ATTN_WIDTH)
    gc = conv_norm_g[0].reshape(1, CONV_WIDTH)
    gf = final_norm_g.reshape(1, D_MODEL)

    x1s, h2s, idxs, ranks, gates = [], [], [], [], []
    counts = jnp.zeros((1, LANES), F32)
    for x in (x_prompt, x_sample):
        B, S, _ = x.shape
        x2d = x.reshape(B * S, D_MODEL)
        qkv, gates3 = _in_projection(x2d, S, g1, w_in_b, _rope_tables(S))
        attn = _band_attention(qkv, B, S, ga)
        x1, h2, idx, gate, rank, counts = _mixer_out(
            x2d, S, attn, gates3, conv_w[0], gc, w_out_b, g2, wr_hi, wr_both, br, counts)
        x1s.append(x1)
        h2s.append(h2)
        idxs.append(idx[:, :TOP_K])
        ranks.append(rank[:, :TOP_K])
        gates.append(gate)

    slot, notfull, te, tx, nsub, first, P = _route(
        jnp.concatenate(idxs, axis=0), jnp.concatenate(ranks, axis=0), counts[0, :N_EXPERTS].astype(jnp.int32))
    slots = []
    t0 = 0
    for h2 in h2s:
        slots.append(slot[t0:t0 + h2.shape[0]])
        t0 += h2.shape[0]
    xs = _dispatch(h2s[0], h2s[1], slot, notfull, P)
    y = _expert_ffn(xs, te, tx, nsub, first, w_gate[0], b_gate[0], w_up[0], b_up[0], w_down[0], b_down[0])
    return tuple(_combine_final(x1, gate, sl, y, gf).reshape(x.shape)
                 for x, x1, gate, sl in zip((x_prompt, x_sample), x1s, gates, slots))
```

```python
import functools

import jax
import jax.numpy as jnp
from jax import lax
from jax.experimental import pallas as pl
from jax.experimental.pallas import tpu as pltpu

F32 = jnp.float32
BF16 = jnp.bfloat16

D_MODEL = 2048
HEAD_DIM = 128
N_HEADS = 8
ATTN_WIDTH = N_HEADS * HEAD_DIM
CONV_WIDTH = D_MODEL - ATTN_WIDTH
N_CONV_GROUPS = 8
QKV_WIDTH = 3 * ATTN_WIDTH
GATE_WIDTH = 3 * CONV_WIDTH
DILATED_PATTERNS = ((128, 1), (512, 4), (2048, 16))
ROPE_THETA = 500000.0
ROPE_DIM = HEAD_DIM // 4
ROPE_HALF = ROPE_DIM // 2
N_EXPERTS = 32
TOP_K = 4
D_FF = D_MODEL
SWIGLU_LIMIT = 7.0
SWIGLU_ALPHA = 1.702
EPS = 1e-5

LANES = 128
BF16_SUBLANES = 16
VMEM_LIMIT = 56 * 1024 * 1024

IN_TM = 1024
IN_TN = 1024
ATTN_QB = 128
ATTN_UNROLL = 8
ATTN_FIN = 256
MIX_TM = 512
DISP_R = 256
MOE_SUB = 256
MOE_TILE = 512
MOE_NSUB = MOE_TILE // MOE_SUB
MOE_TF = 256
FIN_TM = 256


def _params(sem):
    return pltpu.CompilerParams(dimension_semantics=sem, vmem_limit_bytes=VMEM_LIMIT)


def _normed(x_ref, g_ref, h_scr):
    x = x_ref[...]
    ms = jnp.mean(x * x, axis=-1, keepdims=True)
    h_scr[...] = (x * lax.rsqrt(ms + EPS) * g_ref[...]).astype(BF16)


def _qkv_body(x_ref, g_ref, w_ref, c_ref, sa_ref, sb_ref, o_ref, h_scr):
    pl.when(pl.program_id(1) == 0)(functools.partial(_normed, x_ref, g_ref, h_scr))
    acc = jnp.dot(h_scr[...], w_ref[...], preferred_element_type=F32)
    c, sa, sb = c_ref[...], sa_ref[...], sb_ref[...]
    for h in range(N_HEADS):
        sl = slice(h * HEAD_DIM, (h + 1) * HEAD_DIM)
        a = acc[:, sl]
        o_ref[:, sl] = a * c + pltpu.roll(a, HEAD_DIM - ROPE_HALF, 1) * sa + pltpu.roll(a, ROPE_HALF, 1) * sb


def _gates_body(x_ref, g_ref, w_ref, o_ref, h_scr):
    pl.when(pl.program_id(1) == 0)(functools.partial(_normed, x_ref, g_ref, h_scr))
    o_ref[...] = jnp.dot(h_scr[...], w_ref[...], preferred_element_type=F32).astype(BF16)


def _rope_tables(S):
    inv_freq = ROPE_THETA ** (-jnp.arange(ROPE_HALF, dtype=F32) * 2.0 / ROPE_DIM)
    ang = jnp.arange(S, dtype=F32)[:, None] * inv_freq[None, :]
    cos, sin = jnp.cos(ang), jnp.sin(ang)
    pad = jnp.zeros((S, HEAD_DIM - ROPE_DIM), F32)
    zero = jnp.zeros((S, ROPE_HALF), F32)
    c = jnp.concatenate([cos, cos, pad + 1.0], axis=1)
    sa = jnp.concatenate([-sin, zero, pad], axis=1)
    sb = jnp.concatenate([zero, sin, pad], axis=1)
    scale = HEAD_DIM ** -0.5
    none = jnp.zeros_like(c)
    return (jnp.stack([c * scale, c, none + 1.0]), jnp.stack([sa * scale, sa, none]),
            jnp.stack([sb * scale, sb, none]))


def _in_projection(x2d, S, gain, w_bf16, tables):
    T = x2d.shape[0]
    tm = IN_TM
    pos_blocks = S // tm
    n_qkv = QKV_WIDTH // IN_TN
    x_spec = pl.BlockSpec((tm, D_MODEL), lambda i, j: (i, 0))
    g_spec = pl.BlockSpec((1, D_MODEL), lambda i, j: (0, 0))
    out_spec = pl.BlockSpec((tm, IN_TN), lambda i, j: (i, j))
    tab_spec = pl.BlockSpec((None, tm, HEAD_DIM), lambda i, j: (j, i % pos_blocks, 0))
    common = dict(scratch_shapes=[pltpu.VMEM((tm, D_MODEL), BF16)],
                  compiler_params=_params(("arbitrary", "arbitrary")))
    qkv = pl.pallas_call(
        _qkv_body,
        grid=(T // tm, n_qkv),
        in_specs=[x_spec, g_spec, pl.BlockSpec((D_MODEL, IN_TN), lambda i, j: (0, j)), tab_spec, tab_spec, tab_spec],
        out_specs=out_spec,
        out_shape=jax.ShapeDtypeStruct((T, QKV_WIDTH), F32),
        name="qkv_projection", **common,
    )(x2d, gain, w_bf16, *tables)
    gates = pl.pallas_call(
        _gates_body,
        grid=(T // tm, GATE_WIDTH // IN_TN),
        in_specs=[x_spec, g_spec, pl.BlockSpec((D_MODEL, IN_TN), lambda i, j: (0, n_qkv + j))],
        out_specs=out_spec,
        out_shape=jax.ShapeDtypeStruct((T, GATE_WIDTH), BF16),
        name="gate_projection", **common,
    )(x2d, gain, w_bf16)
    return qkv, gates


def _attn_configs(S):
    cfgs, tables = [], []
    for window, d in DILATED_PATTERNS:
        L = S // d
        qb = min(ATTN_QB, L)
        half = window // (2 * d)
        W = min(qb + 2 * half, L)
        assert L % qb == 0 and window % (2 * d) == 0
        assert (W == qb + 2 * half and L // qb >= 2) or (L == qb == W)
        if (qb, W, half) not in tables:
            tables.append((qb, W, half))
        cfgs.append((d, L, qb, W, half, tables.index((qb, W, half))))
    return cfgs, tables


def _mask_table(qb, W, half):
    rel = jnp.arange(W, dtype=jnp.int32)[None, :] - jnp.arange(qb, dtype=jnp.int32)[:, None]
    offs = jnp.array([half, 0, -half], jnp.int32)[:, None, None]
    u = rel[None] + offs
    return jnp.where((u >= 0) & (u <= 2 * half), 0.0, -1e30).astype(F32)


def _attn_body(*refs, S):
    cfgs, tables = _attn_configs(S)
    q_ref, k_ref, v_ref, gain_ref = refs[:4]
    mask_refs = refs[4:4 + len(tables)]
    o_ref, qc, kc, vc, og, lg, sc, mc, q32, k32, v32 = refs[4 + len(tables):]
    srcs, stages, f32s = (q_ref, k_ref, v_ref), (qc, kc, vc), (q32, k32, v32)
    d_prev = 1
    for g, (d, L, qb, W, half, tab) in enumerate(cfgs):
        n_blocks = L // qb
        mask_ref = mask_refs[tab]

        def rows_of_class(r, start, size, d=d):
            if d == 1:
                return pl.ds(start, size)
            return pl.ds(r + start * d, size, stride=d)

        from_prev = d_prev > 1 and d % d_prev == 0
        keep_f32 = any(c[0] > d and c[0] % d == 0 for c in cfgs) and d > 1
        step = d // d_prev if from_prev else d

        def stage(r, carry, d=d, L=L, qb=qb, from_prev=from_prev, keep_f32=keep_f32, step=step, d_prev=d_prev):
            dst = pl.ds(pl.multiple_of(r * L, qb), L)
            for src, st, f32 in zip(srcs, stages, f32s):
                if from_prev:
                    start = (r % d_prev) * (L * step) + r // d_prev
                    x = f32[pl.ds(start, L, stride=step), :]
                elif d == 1:
                    x = src[pl.ds(0, L), :]
                else:
                    x = src[pl.ds(r, L, stride=d), :]
                if keep_f32:
                    f32[dst, :] = x
                st[dst, :] = x.astype(BF16)
            return carry

        if d == 1:
            stage(0, 0)
        else:
            lax.fori_loop(0, d, stage, 0)
        d_prev = d

        def place(nb, L=L, qb=qb, half=half, W=W, n_blocks=n_blocks):
            r = nb // n_blocks
            q0 = pl.multiple_of((nb % n_blocks) * qb, qb)
            ws = jnp.clip(q0 - half, 0, L - W)
            case = jnp.where(q0 == 0, 0, jnp.where(q0 == L - qb, 2, 1))
            keys = pl.ds(pl.multiple_of(r * L + ws, BF16_SUBLANES), W)
            return r, q0, keys, case

        def scores(nb, c, g=g, L=L, qb=qb, W=W, mask_ref=mask_ref, place=place, rows_of_class=rows_of_class):
            r, q0, keys, case = place(nb)
            q = qc[pl.ds(pl.multiple_of(r * L + q0, qb), qb), :]
            s = lax.dot_general(q, kc[keys, :], (((1,), (1,)), ((), ())), preferred_element_type=F32)
            s = s + mask_ref[case]
            m = jnp.max(s, axis=-1, keepdims=True)
            sc[nb, 0:qb, 0:W] = s - m
            mc[pl.ds(pl.multiple_of(nb * qb, qb), qb), :] = jnp.broadcast_to(m, (qb, LANES))
            return c

        def values(nb, c, g=g, qb=qb, W=W, place=place, rows_of_class=rows_of_class):
            r, q0, keys, _ = place(nb)
            p = jnp.exp(sc[nb, 0:qb, 0:W])
            l = jnp.sum(p, axis=-1, keepdims=True)
            acc = jnp.dot(p.astype(BF16), vc[keys, :], preferred_element_type=F32)
            og[g, rows_of_class(r, q0, qb), :] = acc / l
            lg[g, rows_of_class(r, q0, qb), :] = mc[pl.ds(pl.multiple_of(nb * qb, qb), qb), :] + jnp.log(l)
            return c

        unroll = min(ATTN_UNROLL, S // qb)
        lax.fori_loop(0, S // qb, scores, 0, unroll=unroll)
        lax.fori_loop(0, S // qb, values, 0, unroll=unroll)

    def combine(c, carry):
        rows = pl.ds(pl.multiple_of(c * ATTN_FIN, ATTN_FIN), ATTN_FIN)
        l0, l1, l2 = lg[0, rows, :], lg[1, rows, :], lg[2, rows, :]
        mx = jnp.maximum(jnp.maximum(l0, l1), l2)
        e0, e1, e2 = jnp.exp(l0 - mx), jnp.exp(l1 - mx), jnp.exp(l2 - mx)
        a = (e0 * og[0, rows, :] + e1 * og[1, rows, :] + e2 * og[2, rows, :]) / (e0 + e1 + e2)
        ms = jnp.mean(a * a, axis=-1, keepdims=True)
        o_ref[rows, :] = (a * lax.rsqrt(ms + EPS) * gain_ref[...]).astype(BF16)
        return carry

    lax.fori_loop(0, S // ATTN_FIN, combine, 0)


def _band_attention(qkv, B, S, gain):
    cfgs, tables = _attn_configs(S)
    score_shape = (max(S // c[2] for c in cfgs), max(c[2] for c in cfgs), max(c[3] for c in cfgs))
    head = lambda t: pl.BlockSpec((S, HEAD_DIM), lambda b, h: (b, t * N_HEADS + h))
    masks = [_mask_table(*t) for t in tables]
    mask_specs = [pl.BlockSpec(m.shape, lambda b, h: (0, 0, 0)) for m in masks]
    return pl.pallas_call(
        functools.partial(_attn_body, S=S),
        grid=(B, N_HEADS),
        in_specs=[head(0), head(1), head(2), pl.BlockSpec((1, HEAD_DIM), lambda b, h: (0, h))] + mask_specs,
        out_specs=pl.BlockSpec((S, HEAD_DIM), lambda b, h: (b, h)),
        out_shape=jax.ShapeDtypeStruct((B * S, ATTN_WIDTH), BF16),
        scratch_shapes=[pltpu.VMEM((S, HEAD_DIM), BF16)] * 3
        + [pltpu.VMEM((len(DILATED_PATTERNS), S, HEAD_DIM), F32)] * 2
        + [pltpu.VMEM(score_shape, F32)] + [pltpu.VMEM((S, HEAD_DIM), F32)] * 4,
        compiler_params=_params(("arbitrary", "arbitrary")),
        name="band_attention",
    )(qkv, qkv, qkv, gain, *masks)


def _mix_body(attn_ref, cb_ref, cc_ref, cx_ref, ccp_ref, cxp_ref, ccn_ref, cxn_ref, x_ref, wout_ref,
              convw_ref, gc_ref, g2_ref, wrh_ref, wrb_ref, br_ref, tri_ref, base_ref,
              x1_ref, h2_ref, idx_ref, gate_ref, rank_ref, cnt_ref, conv_scr, cnt_scr, *, tm, S, n):
    s = pl.program_id(0)
    live = s >= 1
    buf_a = s % 2
    buf_b = 1 - buf_a

    @pl.when(s == 0)
    def _():
        conv_scr[1] = jnp.zeros((tm, CONV_WIDTH), BF16)
        cnt_scr[...] = base_ref[...]

    mix = (jnp.dot(attn_ref[...], wout_ref[0:ATTN_WIDTH, :], preferred_element_type=F32)
           + jnp.dot(conv_scr[buf_b], wout_ref[ATTN_WIDTH:D_MODEL, :], preferred_element_type=F32))
    x1 = x_ref[...] + mix
    x1_ref[...] = x1
    ms = jnp.mean(x1 * x1, axis=-1, keepdims=True)
    h2 = x1 * lax.rsqrt(ms + EPS) * g2_ref[...]
    hi = h2.astype(BF16)
    hi_f = hi.astype(F32)
    h2_ref[...] = hi_f

    lo = (h2 - hi_f).astype(BF16)
    hi_both = jnp.dot(hi, wrb_ref[...], preferred_element_type=F32)
    logits = (hi_both[:, :LANES] + hi_both[:, LANES:]
              + jnp.dot(lo, wrh_ref[...], preferred_element_type=F32)) + br_ref[...]

    lane = lax.broadcasted_iota(jnp.int32, (tm, LANES), 1)
    lane_f = lane.astype(F32)
    vals, ids = [], []
    for _ in range(TOP_K):
        m = jnp.max(logits, axis=-1, keepdims=True)
        ix = jnp.min(jnp.where(logits == m, lane_f, float(LANES)), axis=-1, keepdims=True)
        vals.append(m)
        ids.append(ix)
        logits = jnp.where(lane_f == ix, -jnp.inf, logits)
    es = [jnp.exp(v - vals[0]) for v in vals]
    den = es[0] + es[1] + es[2] + es[3]
    idx_tile = jnp.zeros((tm, LANES), F32)
    gate_tile = jnp.zeros((tm, LANES), F32)
    for k in range(TOP_K):
        idx_tile = jnp.where(lane == k, ids[k], idx_tile)
        gate_tile = jnp.where(lane == k, es[k] / den, gate_tile)
    idx_ref[...] = idx_tile.astype(jnp.int32)
    gate_ref[...] = gate_tile

    chosen = [lane_f == ids[k] for k in range(TOP_K)]
    picks = sum(c.astype(F32) for c in chosen) * live.astype(F32)
    before = cnt_scr[...] + jnp.dot(tri_ref[...], picks.astype(BF16), preferred_element_type=F32)
    rank_tile = jnp.zeros((tm, LANES), F32)
    for k in range(TOP_K):
        rk = jnp.sum(jnp.where(chosen[k], before, 0.0), axis=-1, keepdims=True)
        rank_tile = jnp.where(lane == k, rk, rank_tile)
    rank_ref[...] = rank_tile.astype(jnp.int32)
    cnt_scr[...] += jnp.sum(picks, axis=0, keepdims=True)
    cnt_ref[...] = cnt_scr[...]

    row = lax.broadcasted_iota(jnp.int32, (tm, LANES), 0)
    r0 = jnp.minimum(s, n - 1) * tm
    at_start = (r0 % S) == 0
    at_end = ((r0 + tm) % S) == 0
    last = BF16_SUBLANES - 1
    gw = CONV_WIDTH // N_CONV_GROUPS
    for c in range(N_CONV_GROUPS):
        sl = slice(c * gw, (c + 1) * gw)
        u = cc_ref[:, sl].astype(F32) * cx_ref[:, sl].astype(F32)
        up = ccp_ref[last:last + 1, sl].astype(F32) * cxp_ref[last:last + 1, sl].astype(F32)
        un = ccn_ref[0:1, sl].astype(F32) * cxn_ref[0:1, sl].astype(F32)
        up = jnp.where(at_start, 0.0, up)
        un = jnp.where(at_end, 0.0, un)
        u_prev = jnp.where(row == 0, up, pltpu.roll(u, 1, 0))
        u_next = jnp.where(row == tm - 1, un, pltpu.roll(u, tm - 1, 0))
        y = convw_ref[0:1, sl] * u_prev + convw_ref[1:2, sl] * u + convw_ref[2:3, sl] * u_next
        cv = cb_ref[:, sl].astype(F32) * y
        ms = jnp.mean(cv * cv, axis=-1, keepdims=True)
        conv_scr[buf_a, :, sl] = (cv * lax.rsqrt(ms + EPS) * gc_ref[:, sl]).astype(BF16)


def _mixer_out(x2d, S, attn, gates3, conv_w, gc, w_out, g2, wr_hi, wr_both, br, base_counts):
    T = x2d.shape[0]
    tm = MIX_TM
    tri = (jnp.arange(tm)[:, None] > jnp.arange(tm)[None, :]).astype(BF16)
    hr = BF16_SUBLANES
    n_halo = T // hr
    n = T // tm
    done = lambda s: jnp.maximum(s - 1, 0)
    conv = lambda s: jnp.minimum(s, n - 1)
    row_tile = lambda w: pl.BlockSpec((tm, w), lambda s: (done(s), 0))
    gate_tile = lambda c: pl.BlockSpec((tm, CONV_WIDTH), lambda s: (conv(s), c))
    prev_halo = lambda c: pl.BlockSpec(
        (hr, CONV_WIDTH), lambda s: (jnp.maximum(conv(s) * (tm // hr) - 1, 0), c))
    next_halo = lambda c: pl.BlockSpec(
        (hr, CONV_WIDTH), lambda s: (jnp.minimum((conv(s) + 1) * (tm // hr), n_halo - 1), c))
    const = lambda shape: pl.BlockSpec(shape, lambda s: (0, 0))
    cb_col, cc_col, cx_col = 0, 1, 2
    return pl.pallas_call(
        functools.partial(_mix_body, tm=tm, S=S, n=n),
        grid=(n + 1,),
        in_specs=[
            row_tile(ATTN_WIDTH),
            gate_tile(cb_col), gate_tile(cc_col), gate_tile(cx_col),
            prev_halo(cc_col), prev_halo(cx_col), next_halo(cc_col), next_halo(cx_col),
            row_tile(D_MODEL),
            const((D_MODEL, D_MODEL)),
            const((3, CONV_WIDTH)), const((1, CONV_WIDTH)), const((1, D_MODEL)),
            const((D_MODEL, LANES)), const((D_MODEL, 2 * LANES)), const((1, LANES)),
            const((tm, tm)), const((1, LANES)),
        ],
        out_specs=[row_tile(D_MODEL), row_tile(D_MODEL), row_tile(LANES), row_tile(LANES), row_tile(LANES),
                   const((1, LANES))],
        out_shape=[
            jax.ShapeDtypeStruct((T, D_MODEL), F32),
            jax.ShapeDtypeStruct((T, D_MODEL), F32),
            jax.ShapeDtypeStruct((T, LANES), jnp.int32),
            jax.ShapeDtypeStruct((T, LANES), F32),
            jax.ShapeDtypeStruct((T, LANES), jnp.int32),
            jax.ShapeDtypeStruct((1, LANES), F32),
        ],
        scratch_shapes=[pltpu.VMEM((2, tm, CONV_WIDTH), BF16), pltpu.VMEM((1, LANES), F32)],
        compiler_params=_params(("arbitrary",)),
        name="mixer_out_router",
    )(attn, gates3, gates3, gates3, gates3, gates3, gates3, gates3, x2d, w_out, conv_w,
      gc, g2, wr_hi, wr_both, br, tri, base_counts)


def _dispatch_body(nf_ref, slot_ref, ha_ref, hb_ref, xs_hbm, zero_scr, zsem, sem, *, steps_a):
    i = pl.program_id(0)
    n_sub = xs_hbm.shape[0] // MOE_SUB

    def zero_copy(sb):
        return pltpu.make_async_copy(zero_scr, xs_hbm.at[pl.ds(sb * MOE_SUB, MOE_SUB), :], zsem)

    @pl.when(i == 0)
    def _():
        zero_scr[...] = jnp.zeros_like(zero_scr)

        def start(sb, cnt):
            pl.when(nf_ref[sb] != 0)(lambda: zero_copy(sb).start())
            return cnt + nf_ref[sb]

        cnt = lax.fori_loop(0, n_sub, start, 0)

        def wait(_, c):
            zero_copy(0).wait()
            return c

        lax.fori_loop(0, cnt, wait, 0)

    def scatter(h_ref):
        for j in range(DISP_R):
            for k in range(TOP_K):
                s = slot_ref[0, j * TOP_K + k]
                pltpu.make_async_copy(h_ref.at[pl.ds(j, 1), :], xs_hbm.at[pl.ds(s, 1), :], sem).start(priority=k % 2)

    pl.when(i < steps_a)(lambda: scatter(ha_ref))
    pl.when(i >= steps_a)(lambda: scatter(hb_ref))
    n_rows = DISP_R * TOP_K
    pltpu.make_async_copy(xs_hbm.at[pl.ds(0, n_rows), :], xs_hbm.at[pl.ds(0, n_rows), :], sem).wait()


def _dispatch(h2_a, h2_b, slot, notfull, P):
    steps_a = h2_a.shape[0] // DISP_R
    steps_b = h2_b.shape[0] // DISP_R
    n_steps = steps_a + steps_b
    grid_spec = pltpu.PrefetchScalarGridSpec(
        num_scalar_prefetch=1,
        grid=(n_steps,),
        in_specs=[
            pl.BlockSpec((None, 1, DISP_R * TOP_K), lambda i, nf: (i, 0, 0), memory_space=pltpu.SMEM),
            pl.BlockSpec((DISP_R, D_MODEL), lambda i, nf: (jnp.minimum(i, steps_a - 1), 0)),
            pl.BlockSpec((DISP_R, D_MODEL), lambda i, nf: (jnp.maximum(i - steps_a, 0), 0)),
        ],
        out_specs=pl.BlockSpec(memory_space=pl.ANY),
        scratch_shapes=[pltpu.VMEM((MOE_SUB, D_MODEL), F32), pltpu.SemaphoreType.DMA(()),
                        pltpu.SemaphoreType.DMA(())],
    )
    return pl.pallas_call(
        functools.partial(_dispatch_body, steps_a=steps_a),
        grid_spec=grid_spec,
        out_shape=jax.ShapeDtypeStruct((P, D_MODEL), F32),
        compiler_params=_params(("arbitrary",)),
        name="dispatch",
    )(notfull, slot.reshape(n_steps, 1, DISP_R * TOP_K), h2_a, h2_b)


def _expert_body(te_ref, tx_ref, ns_ref, first_ref, x_ref, bg_ref, bu_ref, bd_ref, wg_hbm, wu_hbm, wd_hbm,
                 y_ref, xb, wg_res, wu_res, wd_res, stg_g, stg_u, stg_d, sems):
    i = pl.program_id(0)
    e = te_ref[i]
    nsub = ns_ref[i]
    first = first_ref[i]
    nf = D_FF // MOE_TF
    kh = D_MODEL // 2
    fh = MOE_TF // 2

    full_later_tile = (first == 0) & (nsub == MOE_NSUB)

    @pl.when(jnp.logical_not(full_later_tile))
    def _():
        for s in range(MOE_NSUB):
            rows = slice(s * MOE_SUB, (s + 1) * MOE_SUB)
            y_ref[rows, :] = jnp.where(s < nsub, jnp.broadcast_to(bd_ref[...], (MOE_SUB, D_MODEL)), 0.0)

    @pl.when(nsub > 0)
    def _():
        xb[...] = x_ref[...].astype(BF16)

    def sub_block(s, f, from_bias=False, n_sub=1):
        rows = slice(s * MOE_SUB, (s + n_sub) * MOE_SUB)
        x = xb[rows, :]
        g = jnp.dot(x, wg_res[f], preferred_element_type=F32) + bg_ref[f]
        u = jnp.dot(x, wu_res[f], preferred_element_type=F32) + bu_ref[f]
        g = jnp.minimum(g, SWIGLU_LIMIT)
        u = jnp.clip(u, -SWIGLU_LIMIT, SWIGLU_LIMIT)
        a = (u + 1.0) * (g * jax.nn.sigmoid(SWIGLU_ALPHA * g))
        dn = jnp.dot(a.astype(BF16), wd_res[f], preferred_element_type=F32)
        if from_bias:
            y_ref[rows, :] = bd_ref[...] + dn
        else:
            y_ref[rows, :] += dn

    def half_copies(f, h):
        cols = pl.ds(f * MOE_TF, MOE_TF)
        drows = pl.ds(f * MOE_TF + h * fh, fh)
        return (pltpu.make_async_copy(wg_hbm.at[e, pl.ds(h * kh, kh), cols], stg_g.at[h], sems.at[0, h]),
                pltpu.make_async_copy(wu_hbm.at[e, pl.ds(h * kh, kh), cols], stg_u.at[h], sems.at[1, h]),
                pltpu.make_async_copy(wd_hbm.at[e, drows, :], stg_d.at[h], sems.at[2, h]))

    @pl.when(first == 1)
    def _():
        for h in range(2):
            for c in half_copies(0, h):
                c.start()

        def load_and_compute(f, carry):
            for h in range(2):
                for c in half_copies(f, h):
                    c.wait()
                wg_res[f, h * kh:(h + 1) * kh, :] = stg_g[h].astype(BF16)
                wu_res[f, h * kh:(h + 1) * kh, :] = stg_u[h].astype(BF16)
                wd_res[f, h * fh:(h + 1) * fh, :] = stg_d[h].astype(BF16)

                @pl.when(f + 1 < nf)
                def _():
                    for c in half_copies(f + 1, h):
                        c.start()
            pl.when(nsub == MOE_NSUB)(functools.partial(sub_block, 0, f, n_sub=MOE_NSUB))
            for s in range(MOE_NSUB - 1):
                pl.when((s < nsub) & (nsub < MOE_NSUB))(functools.partial(sub_block, s, f))
            return carry

        lax.fori_loop(0, nf, load_and_compute, 0)

    @pl.when(full_later_tile)
    def _():
        sub_block(0, 0, from_bias=True, n_sub=MOE_NSUB)

        def full(f, carry):
            sub_block(0, f, n_sub=MOE_NSUB)
            return carry
        lax.fori_loop(1, nf, full, 0, unroll=4)

    @pl.when((first == 0) & (nsub > 0) & (nsub < MOE_NSUB))
    def _():
        def part(f, carry):
            for s in range(MOE_NSUB - 1):
                pl.when(s < nsub)(functools.partial(sub_block, s, f))
            return carry
        lax.fori_loop(0, nf, part, 0)


def _expert_ffn(xs, tile_expert, tile_xblk, tile_nsub, tile_first, w_gate, b_gate, w_up, b_up, w_down, b_down):
    P = xs.shape[0]
    n_tiles = P // MOE_TILE
    nf = D_FF // MOE_TF
    per_expert = lambda shape: pl.BlockSpec(
        (None,) + shape, lambda i, te, tx, ns, fi: (te[i],) + (0,) * len(shape))
    grid_spec = pltpu.PrefetchScalarGridSpec(
        num_scalar_prefetch=4,
        grid=(n_tiles,),
        in_specs=[
            pl.BlockSpec((MOE_TILE, D_MODEL), lambda i, te, tx, ns, fi: (tx[i], 0)),
            per_expert((nf, 1, MOE_TF)), per_expert((nf, 1, MOE_TF)), per_expert((1, D_MODEL)),
            pl.BlockSpec(memory_space=pl.ANY), pl.BlockSpec(memory_space=pl.ANY), pl.BlockSpec(memory_space=pl.ANY),
        ],
        out_specs=pl.BlockSpec((MOE_TILE, D_MODEL), lambda i, te, tx, ns, fi: (i, 0)),
        scratch_shapes=[
            pltpu.VMEM((MOE_TILE, D_MODEL), BF16),
            pltpu.VMEM((nf, D_MODEL, MOE_TF), BF16), pltpu.VMEM((nf, D_MODEL, MOE_TF), BF16),
            pltpu.VMEM((nf, MOE_TF, D_MODEL), BF16),
            pltpu.VMEM((2, D_MODEL // 2, MOE_TF), F32), pltpu.VMEM((2, D_MODEL // 2, MOE_TF), F32),
            pltpu.VMEM((2, MOE_TF // 2, D_MODEL), F32),
            pltpu.SemaphoreType.DMA((3, 2)),
        ],
    )
    return pl.pallas_call(
        _expert_body,
        grid_spec=grid_spec,
        out_shape=jax.ShapeDtypeStruct((P, D_MODEL), F32),
        compiler_params=_params(("arbitrary",)),
        name="expert_ffn",
    )(tile_expert, tile_xblk, tile_nsub, tile_first, xs,
      b_gate.reshape(N_EXPERTS, nf, 1, MOE_TF), b_up.reshape(N_EXPERTS, nf, 1, MOE_TF),
      b_down.reshape(N_EXPERTS, 1, D_MODEL), w_gate, w_up, w_down)


def _final_body(slot_ref, slot_next_ref, x1_ref, gate_ref, gf_ref, y_hbm, o_ref, ybuf, sems, *, n):
    i = pl.program_id(0)
    tm = FIN_TM

    def gather(s_ref, b):
        for j in range(tm):
            for k in range(TOP_K):
                s = s_ref[0, j * TOP_K + k]
                pltpu.make_async_copy(y_hbm.at[pl.ds(s, 1), :], ybuf.at[b, k, pl.ds(j, 1), :],
                                      sems.at[b]).start(priority=k % 2)

    b = i % 2
    pl.when(i == 0)(lambda: gather(slot_ref, 0))
    pl.when((i + 1 < n) & (b == 0))(lambda: gather(slot_next_ref, 1))
    pl.when((i + 1 < n) & (b == 1))(lambda: gather(slot_next_ref, 0))
    pltpu.make_async_copy(ybuf.at[b], ybuf.at[b], sems.at[b]).wait()

    gates = gate_ref[...]
    acc = x1_ref[...]
    for k in range(TOP_K):
        acc = acc + gates[:, k:k + 1] * ybuf[b, k]
    ms = jnp.mean(acc * acc, axis=-1, keepdims=True)
    o_ref[...] = acc * lax.rsqrt(ms + EPS) * gf_ref[...]


def _combine_final(x1, gates, slot, y, gf):
    T = x1.shape[0]
    tm = FIN_TM
    n_steps = T // tm
    row_tile = lambda w: pl.BlockSpec((tm, w), lambda i: (i, 0))
    slot3 = slot.reshape(n_steps, 1, tm * TOP_K)
    slot_spec = lambda step: pl.BlockSpec((None, 1, tm * TOP_K), step, memory_space=pltpu.SMEM)
    return pl.pallas_call(
        functools.partial(_final_body, n=n_steps),
        grid=(n_steps,),
        in_specs=[
            slot_spec(lambda i: (i, 0, 0)),
            slot_spec(lambda i: (jnp.minimum(i + 1, n_steps - 1), 0, 0)),
            row_tile(D_MODEL), row_tile(LANES),
            pl.BlockSpec((1, D_MODEL), lambda i: (0, 0)),
            pl.BlockSpec(memory_space=pl.ANY),
        ],
        out_specs=row_tile(D_MODEL),
        out_shape=jax.ShapeDtypeStruct((T, D_MODEL), F32),
        scratch_shapes=[pltpu.VMEM((2, TOP_K, tm, D_MODEL), F32), pltpu.SemaphoreType.DMA((2,))],
        compiler_params=_params(("arbitrary",)),
        name="combine_final_norm",
    )(slot3, slot3, x1, gates, gf, y)


def _route(idx, rank, counts):
    n = idx.shape[0] * TOP_K
    n_tiles = n // MOE_TILE + N_EXPERTS
    tiles_per_e = (counts + MOE_TILE - 1) // MOE_TILE
    tile_end = jnp.cumsum(tiles_per_e)
    tile_start = tile_end - tiles_per_e
    experts = jnp.arange(N_EXPERTS, dtype=jnp.int32)
    region = jnp.sum(jnp.where(idx[..., None] == experts, tile_start * MOE_TILE, 0), axis=-1)
    slot = region + rank

    t = jnp.arange(n_tiles, dtype=jnp.int32)
    used = t < tile_end[-1]
    last_used = jnp.maximum(tile_end[-1] - 1, 0)
    tt = jnp.where(used, t, last_used)
    te = jnp.minimum(jnp.sum(tt[:, None] >= tile_end[None, :], axis=1), N_EXPERTS - 1).astype(jnp.int32)
    rows = jnp.where(used, jnp.clip(counts[te] - (tt - tile_start[te]) * MOE_TILE, 0, MOE_TILE), 0)
    nsub = ((rows + MOE_SUB - 1) // MOE_SUB).astype(jnp.int32)
    sub_end = (jnp.arange(MOE_NSUB, dtype=jnp.int32) + 1) * MOE_SUB
    notfull = (rows[:, None] < sub_end[None, :]).astype(jnp.int32).reshape(-1)
    first = (used & (t == tile_start[te])).astype(jnp.int32)
    return slot, notfull, te, tt.astype(jnp.int32), nsub, first, n_tiles * MOE_TILE


def kernel(x_prompt, x_sample, norm1_g, w_in, conv_w, attn_norm_g, conv_norm_g, w_out, norm2_g,
           w_router, b_router, w_gate, b_gate, w_up, b_up, w_down, b_down, final_norm_g):
    assert norm1_g.shape[0] == 1, "one layer"
    w_in_b = w_in[0].astype(BF16)
    w_out_b = w_out[0].astype(BF16)
    wr = jnp.pad(w_router[0], ((0, 0), (0, LANES - N_EXPERTS)))
    wr_hi = wr.astype(BF16)
    wr_both = jnp.concatenate([wr_hi, (wr - wr_hi.astype(F32)).astype(BF16)], axis=1)
    br = jnp.pad(b_router[0], (0, LANES - N_EXPERTS), constant_values=-jnp.inf).reshape(1, LANES)
    g1 = norm1_g[0].reshape(1, D_MODEL)
    g2 = norm2_g[0].reshape(1, D_MODEL)
    ga = attn_norm_g[0].reshape(1, ATTN_WIDTH)
    gc = conv_norm_g[0].reshape(1, CONV_WIDTH)
    gf = final_norm_g.reshape(1, D_MODEL)

    x1s, h2s, idxs, ranks, gates = [], [], [], [], []
    counts = jnp.zeros((1, LANES), F32)
    for x in (x_prompt, x_sample):
        B, S, _ = x.shape
        x2d = x.reshape(B * S, D_MODEL)
        qkv, gates3 = _in_projection(x2d, S, g1, w_in_b, _rope_tables(S))
        attn = _band_attention(qkv, B, S, ga)
        x1, h2, idx, gate, rank, counts = _mixer_out(
            x2d, S, attn, gates3, conv_w[0], gc, w_out_b, g2, wr_hi, wr_both, br, counts)
        x1s.append(x1)
        h2s.append(h2)
        idxs.append(idx[:, :TOP_K])
        ranks.append(rank[:, :TOP_K])
        gates.append(gate)

    slot, notfull, te, tx, nsub, first, P = _route(
        jnp.concatenate(idxs, axis=0), jnp.concatenate(ranks, axis=0), counts[0, :N_EXPERTS].astype(jnp.int32))
    slots = []
    t0 = 0
    for h2 in h2s:
        slots.append(slot[t0:t0 + h2.shape[0]])
        t0 += h2.shape[0]
    xs = _dispatch(h2s[0], h2s[1], slot, notfull, P)
    y = _expert_ffn(xs, te, tx, nsub, first, w_gate[0], b_gate[0], w_up[0], b_up[0], w_down[0], b_down[0])
    return tuple(_combine_final(x1, gate, sl, y, gf).reshape(x.shape)
                 for x, x1, gate, sl in zip((x_prompt, x_sample), x1s, gates, slots))
```

```python
import functools

import jax
import jax.numpy as jnp
from jax import lax
from jax.experimental import pallas as pl
from jax.experimental.pallas import tpu as pltpu

F32 = jnp.float32
BF16 = jnp.bfloat16

D_MODEL = 2048
HEAD_DIM = 128
N_HEADS = 8
ATTN_WIDTH = N_HEADS * HEAD_DIM
CONV_WIDTH = D_MODEL - ATTN_WIDTH
N_CONV_GROUPS = 8
QKV_WIDTH = 3 * ATTN_WIDTH
GATE_WIDTH = 3 * CONV_WIDTH
DILATED_PATTERNS = ((128, 1), (512, 4), (2048, 16))
ROPE_THETA = 500000.0
ROPE_DIM = HEAD_DIM // 4
ROPE_HALF = ROPE_DIM // 2
N_EXPERTS = 32
TOP_K = 4
D_FF = D_MODEL
SWIGLU_LIMIT = 7.0
SWIGLU_ALPHA = 1.702
EPS = 1e-5

LANES = 128
BF16_SUBLANES = 16
VMEM_LIMIT = 56 * 1024 * 1024

IN_TM = 1024
IN_TN = 1024
ATTN_QB = 128
ATTN_UNROLL = 16
ATTN_FIN = 256
MIX_TM = 512
DISP_R = 256
MOE_SUB = 256
MOE_TILE = 512
MOE_NSUB = MOE_TILE // MOE_SUB
MOE_TF = 256
FIN_TM = 256


def _params(sem):
    return pltpu.CompilerParams(dimension_semantics=sem, vmem_limit_bytes=VMEM_LIMIT)


def _normed(x_ref, g_ref, h_scr):
    x = x_ref[...]
    ms = jnp.mean(x * x, axis=-1, keepdims=True)
    h_scr[...] = (x * lax.rsqrt(ms + EPS) * g_ref[...]).astype(BF16)


def _qkv_body(x_ref, g_ref, w_ref, c_ref, sa_ref, sb_ref, o_ref, h_scr):
    pl.when(pl.program_id(1) == 0)(functools.partial(_normed, x_ref, g_ref, h_scr))
    acc = jnp.dot(h_scr[...], w_ref[...], preferred_element_type=F32)
    c, sa, sb = c_ref[...], sa_ref[...], sb_ref[...]
    for h in range(N_HEADS):
        sl = slice(h * HEAD_DIM, (h + 1) * HEAD_DIM)
        a = acc[:, sl]
        o_ref[:, sl] = a * c + pltpu.roll(a, HEAD_DIM - ROPE_HALF, 1) * sa + pltpu.roll(a, ROPE_HALF, 1) * sb


def _gates_body(x_ref, g_ref, w_ref, o_ref, h_scr):
    pl.when(pl.program_id(1) == 0)(functools.partial(_normed, x_ref, g_ref, h_scr))
    o_ref[...] = jnp.dot(h_scr[...], w_ref[...], preferred_element_type=F32).astype(BF16)


def _rope_tables(S):
    inv_freq = ROPE_THETA ** (-jnp.arange(ROPE_HALF, dtype=F32) * 2.0 / ROPE_DIM)
    ang = jnp.arange(S, dtype=F32)[:, None] * inv_freq[None, :]
    cos, sin = jnp.cos(ang), jnp.sin(ang)
    pad = jnp.zeros((S, HEAD_DIM - ROPE_DIM), F32)
    zero = jnp.zeros((S, ROPE_HALF), F32)
    c = jnp.concatenate([cos, cos, pad + 1.0], axis=1)
    sa = jnp.concatenate([-sin, zero, pad], axis=1)
    sb = jnp.concatenate([zero, sin, pad], axis=1)
    scale = HEAD_DIM ** -0.5
    none = jnp.zeros_like(c)
    return (jnp.stack([c * scale, c, none + 1.0]), jnp.stack([sa * scale, sa, none]),
            jnp.stack([sb * scale, sb, none]))


def _in_projection(x2d, S, gain, w_bf16, tables):
    T = x2d.shape[0]
    tm = IN_TM
    pos_blocks = S // tm
    n_qkv = QKV_WIDTH // IN_TN
    x_spec = pl.BlockSpec((tm, D_MODEL), lambda i, j: (i, 0))
    g_spec = pl.BlockSpec((1, D_MODEL), lambda i, j: (0, 0))
    out_spec = pl.BlockSpec((tm, IN_TN), lambda i, j: (i, j))
    tab_spec = pl.BlockSpec((None, tm, HEAD_DIM), lambda i, j: (j, i % pos_blocks, 0))
    common = dict(scratch_shapes=[pltpu.VMEM((tm, D_MODEL), BF16)],
                  compiler_params=_params(("arbitrary", "arbitrary")))
    qkv = pl.pallas_call(
        _qkv_body,
        grid=(T // tm, n_qkv),
        in_specs=[x_spec, g_spec, pl.BlockSpec((D_MODEL, IN_TN), lambda i, j: (0, j)), tab_spec, tab_spec, tab_spec],
        out_specs=out_spec,
        out_shape=jax.ShapeDtypeStruct((T, QKV_WIDTH), F32),
        name="qkv_projection", **common,
    )(x2d, gain, w_bf16, *tables)
    gates = pl.pallas_call(
        _gates_body,
        grid=(T // tm, GATE_WIDTH // IN_TN),
        in_specs=[x_spec, g_spec, pl.BlockSpec((D_MODEL, IN_TN), lambda i, j: (0, n_qkv + j))],
        out_specs=out_spec,
        out_shape=jax.ShapeDtypeStruct((T, GATE_WIDTH), BF16),
        name="gate_projection", **common,
    )(x2d, gain, w_bf16)
    return qkv, gates


def _attn_configs(S):
    cfgs, tables = [], []
    for window, d in DILATED_PATTERNS:
        L = S // d
        qb = min(ATTN_QB, L)
        half = window // (2 * d)
        W = min(qb + 2 * half, L)
        assert L % qb == 0 and window % (2 * d) == 0
        assert (W == qb + 2 * half and L // qb >= 2) or (L == qb == W)
        if (qb, W, half) not in tables:
            tables.append((qb, W, half))
        cfgs.append((d, L, qb, W, half, tables.index((qb, W, half))))
    return cfgs, tables


def _mask_table(qb, W, half):
    rel = jnp.arange(W, dtype=jnp.int32)[None, :] - jnp.arange(qb, dtype=jnp.int32)[:, None]
    offs = jnp.array([half, 0, -half], jnp.int32)[:, None, None]
    u = rel[None] + offs
    return jnp.where((u >= 0) & (u <= 2 * half), 0.0, -1e30).astype(F32)


def _attn_body(*refs, S):
    cfgs, tables = _attn_configs(S)
    q_ref, k_ref, v_ref, gain_ref = refs[:4]
    mask_refs = refs[4:4 + len(tables)]
    o_ref, qc, kc, vc, og, lg, sc, mc, q32, k32, v32 = refs[4 + len(tables):]
    srcs, stages, f32s = (q_ref, k_ref, v_ref), (qc, kc, vc), (q32, k32, v32)
    d_prev = 1
    for g, (d, L, qb, W, half, tab) in enumerate(cfgs):
        n_blocks = L // qb
        mask_ref = mask_refs[tab]

        def rows_of_class(r, start, size, d=d):
            if d == 1:
                return pl.ds(start, size)
            return pl.ds(r + start * d, size, stride=d)

        from_prev = d_prev > 1 and d % d_prev == 0
        keep_f32 = any(c[0] > d and c[0] % d == 0 for c in cfgs) and d > 1
        step = d // d_prev if from_prev else d

        def stage(r, carry, d=d, L=L, qb=qb, from_prev=from_prev, keep_f32=keep_f32, step=step, d_prev=d_prev):
            dst = pl.ds(pl.multiple_of(r * L, qb), L)
            for src, st, f32 in zip(srcs, stages, f32s):
                if from_prev:
                    start = (r % d_prev) * (L * step) + r // d_prev
                    x = f32[pl.ds(start, L, stride=step), :]
                elif d == 1:
                    x = src[pl.ds(0, L), :]
                else:
                    x = src[pl.ds(r, L, stride=d), :]
                if keep_f32:
                    f32[dst, :] = x
                st[dst, :] = x.astype(BF16)
            return carry

        if d == 1:
            stage(0, 0)
        else:
            lax.fori_loop(0, d, stage, 0)
        d_prev = d

        def place(nb, L=L, qb=qb, half=half, W=W, n_blocks=n_blocks):
            r = nb // n_blocks
            q0 = pl.multiple_of((nb % n_blocks) * qb, qb)
            ws = jnp.clip(q0 - half, 0, L - W)
            case = jnp.where(q0 == 0, 0, jnp.where(q0 == L - qb, 2, 1))
            keys = pl.ds(pl.multiple_of(r * L + ws, BF16_SUBLANES), W)
            return r, q0, keys, case

        def scores(nb, c, g=g, L=L, qb=qb, W=W, mask_ref=mask_ref, place=place, rows_of_class=rows_of_class):
            r, q0, keys, case = place(nb)
            q = qc[pl.ds(pl.multiple_of(r * L + q0, qb), qb), :]
            s = lax.dot_general(q, kc[keys, :], (((1,), (1,)), ((), ())), preferred_element_type=F32)
            s = s + mask_ref[case]
            m = jnp.max(s, axis=-1, keepdims=True)
            sc[nb, 0:qb, 0:W] = s - m
            mc[pl.ds(pl.multiple_of(nb * qb, qb), qb), :] = jnp.broadcast_to(m, (qb, LANES))
            return c

        def values(nb, c, g=g, qb=qb, W=W, place=place, rows_of_class=rows_of_class):
            r, q0, keys, _ = place(nb)
            p = jnp.exp(sc[nb, 0:qb, 0:W])
            l = jnp.sum(p, axis=-1, keepdims=True)
            acc = jnp.dot(p.astype(BF16), vc[keys, :], preferred_element_type=F32)
            og[g, rows_of_class(r, q0, qb), :] = acc / l
            lg[g, rows_of_class(r, q0, qb), :] = mc[pl.ds(pl.multiple_of(nb * qb, qb), qb), :] + jnp.log(l)
            return c

        unroll = min(ATTN_UNROLL, S // qb)
        lax.fori_loop(0, S // qb, scores, 0, unroll=unroll)
        lax.fori_loop(0, S // qb, values, 0, unroll=unroll)

    def combine(c, carry):
        rows = pl.ds(pl.multiple_of(c * ATTN_FIN, ATTN_FIN), ATTN_FIN)
        l0, l1, l2 = lg[0, rows, :], lg[1, rows, :], lg[2, rows, :]
        mx = jnp.maximum(jnp.maximum(l0, l1), l2)
        e0, e1, e2 = jnp.exp(l0 - mx), jnp.exp(l1 - mx), jnp.exp(l2 - mx)
        a = (e0 * og[0, rows, :] + e1 * og[1, rows, :] + e2 * og[2, rows, :]) / (e0 + e1 + e2)
        ms = jnp.mean(a * a, axis=-1, keepdims=True)
        o_ref[rows, :] = (a * lax.rsqrt(ms + EPS) * gain_ref[...]).astype(BF16)
        return carry

    lax.fori_loop(0, S // ATTN_FIN, combine, 0)


def _band_attention(qkv, B, S, gain):
    cfgs, tables = _attn_configs(S)
    score_shape = (max(S // c[2] for c in cfgs), max(c[2] for c in cfgs), max(c[3] for c in cfgs))
    head = lambda t: pl.BlockSpec((S, HEAD_DIM), lambda b, h: (b, t * N_HEADS + h))
    masks = [_mask_table(*t) for t in tables]
    mask_specs = [pl.BlockSpec(m.shape, lambda b, h: (0, 0, 0)) for m in masks]
    return pl.pallas_call(
        functools.partial(_attn_body, S=S),
        grid=(B, N_HEADS),
        in_specs=[head(0), head(1), head(2), pl.BlockSpec((1, HEAD_DIM), lambda b, h: (0, h))] + mask_specs,
        out_specs=pl.BlockSpec((S, HEAD_DIM), lambda b, h: (b, h)),
        out_shape=jax.ShapeDtypeStruct((B * S, ATTN_WIDTH), BF16),
        scratch_shapes=[pltpu.VMEM((S, HEAD_DIM), BF16)] * 3
        + [pltpu.VMEM((len(DILATED_PATTERNS), S, HEAD_DIM), F32)] * 2
        + [pltpu.VMEM(score_shape, F32)] + [pltpu.VMEM((S, HEAD_DIM), F32)] * 4,
        compiler_params=_params(("arbitrary", "arbitrary")),
        name="band_attention",
    )(qkv, qkv, qkv, gain, *masks)


def _mix_body(attn_ref, cb_ref, cc_ref, cx_ref, ccp_ref, cxp_ref, ccn_ref, cxn_ref, x_ref, wout_ref,
              convw_ref, gc_ref, g2_ref, wrh_ref, wrb_ref, br_ref, tri_ref, base_ref,
              x1_ref, h2_ref, idx_ref, gate_ref, rank_ref, cnt_ref, conv_scr, cnt_scr, *, tm, S, n):
    s = pl.program_id(0)
    live = s >= 1
    buf_a = s % 2
    buf_b = 1 - buf_a

    @pl.when(s == 0)
    def _():
        conv_scr[1] = jnp.zeros((tm, CONV_WIDTH), BF16)
        cnt_scr[...] = base_ref[...]

    mix = (jnp.dot(attn_ref[...], wout_ref[0:ATTN_WIDTH, :], preferred_element_type=F32)
           + jnp.dot(conv_scr[buf_b], wout_ref[ATTN_WIDTH:D_MODEL, :], preferred_element_type=F32))
    x1 = x_ref[...] + mix
    x1_ref[...] = x1
    ms = jnp.mean(x1 * x1, axis=-1, keepdims=True)
    h2 = x1 * lax.rsqrt(ms + EPS) * g2_ref[...]
    hi = h2.astype(BF16)
    hi_f = hi.astype(F32)
    h2_ref[...] = hi_f

    lo = (h2 - hi_f).astype(BF16)
    hi_both = jnp.dot(hi, wrb_ref[...], preferred_element_type=F32)
    logits = (hi_both[:, :LANES] + hi_both[:, LANES:]
              + jnp.dot(lo, wrh_ref[...], preferred_element_type=F32)) + br_ref[...]

    lane = lax.broadcasted_iota(jnp.int32, (tm, LANES), 1)
    lane_f = lane.astype(F32)
    vals, ids = [], []
    for _ in range(TOP_K):
        m = jnp.max(logits, axis=-1, keepdims=True)
        ix = jnp.min(jnp.where(logits == m, lane_f, float(LANES)), axis=-1, keepdims=True)
        vals.append(m)
        ids.append(ix)
        logits = jnp.where(lane_f == ix, -jnp.inf, logits)
    es = [jnp.exp(v - vals[0]) for v in vals]
    den = es[0] + es[1] + es[2] + es[3]
    idx_tile = jnp.zeros((tm, LANES), F32)
    gate_tile = jnp.zeros((tm, LANES), F32)
    for k in range(TOP_K):
        idx_tile = jnp.where(lane == k, ids[k], idx_tile)
        gate_tile = jnp.where(lane == k, es[k] / den, gate_tile)
    idx_ref[...] = idx_tile.astype(jnp.int32)
    gate_ref[...] = gate_tile

    chosen = [lane_f == ids[k] for k in range(TOP_K)]
    picks = sum(c.astype(F32) for c in chosen) * live.astype(F32)
    before = cnt_scr[...] + jnp.dot(tri_ref[...], picks.astype(BF16), preferred_element_type=F32)
    rank_tile = jnp.zeros((tm, LANES), F32)
    for k in range(TOP_K):
        rk = jnp.sum(jnp.where(chosen[k], before, 0.0), axis=-1, keepdims=True)
        rank_tile = jnp.where(lane == k, rk, rank_tile)
    rank_ref[...] = rank_tile.astype(jnp.int32)
    cnt_scr[...] += jnp.sum(picks, axis=0, keepdims=True)
    cnt_ref[...] = cnt_scr[...]

    row = lax.broadcasted_iota(jnp.int32, (tm, LANES), 0)
    r0 = jnp.minimum(s, n - 1) * tm
    at_start = (r0 % S) == 0
    at_end = ((r0 + tm) % S) == 0
    last = BF16_SUBLANES - 1
    gw = CONV_WIDTH // N_CONV_GROUPS
    for c in range(N_CONV_GROUPS):
        sl = slice(c * gw, (c + 1) * gw)
        u = cc_ref[:, sl].astype(F32) * cx_ref[:, sl].astype(F32)
        up = ccp_ref[last:last + 1, sl].astype(F32) * cxp_ref[last:last + 1, sl].astype(F32)
        un = ccn_ref[0:1, sl].astype(F32) * cxn_ref[0:1, sl].astype(F32)
        up = jnp.where(at_start, 0.0, up)
        un = jnp.where(at_end, 0.0, un)
        u_prev = jnp.where(row == 0, up, pltpu.roll(u, 1, 0))
        u_next = jnp.where(row == tm - 1, un, pltpu.roll(u, tm - 1, 0))
        y = convw_ref[0:1, sl] * u_prev + convw_ref[1:2, sl] * u + convw_ref[2:3, sl] * u_next
        cv = cb_ref[:, sl].astype(F32) * y
        ms = jnp.mean(cv * cv, axis=-1, keepdims=True)
        conv_scr[buf_a, :, sl] = (cv * lax.rsqrt(ms + EPS) * gc_ref[:, sl]).astype(BF16)


def _mixer_out(x2d, S, attn, gates3, conv_w, gc, w_out, g2, wr_hi, wr_both, br, base_counts):
    T = x2d.shape[0]
    tm = MIX_TM
    tri = (jnp.arange(tm)[:, None] > jnp.arange(tm)[None, :]).astype(BF16)
    hr = BF16_SUBLANES
    n_halo = T // hr
    n = T // tm
    done = lambda s: jnp.maximum(s - 1, 0)
    conv = lambda s: jnp.minimum(s, n - 1)
    row_tile = lambda w: pl.BlockSpec((tm, w), lambda s: (done(s), 0))
    gate_tile = lambda c: pl.BlockSpec((tm, CONV_WIDTH), lambda s: (conv(s), c))
    prev_halo = lambda c: pl.BlockSpec(
        (hr, CONV_WIDTH), lambda s: (jnp.maximum(conv(s) * (tm // hr) - 1, 0), c))
    next_halo = lambda c: pl.BlockSpec(
        (hr, CONV_WIDTH), lambda s: (jnp.minimum((conv(s) + 1) * (tm // hr), n_halo - 1), c))
    const = lambda shape: pl.BlockSpec(shape, lambda s: (0, 0))
    cb_col, cc_col, cx_col = 0, 1, 2
    return pl.pallas_call(
        functools.partial(_mix_body, tm=tm, S=S, n=n),
        grid=(n + 1,),
        in_specs=[
            row_tile(ATTN_WIDTH),
            gate_tile(cb_col), gate_tile(cc_col), gate_tile(cx_col),
            prev_halo(cc_col), prev_halo(cx_col), next_halo(cc_col), next_halo(cx_col),
            row_tile(D_MODEL),
            const((D_MODEL, D_MODEL)),
            const((3, CONV_WIDTH)), const((1, CONV_WIDTH)), const((1, D_MODEL)),
            const((D_MODEL, LANES)), const((D_MODEL, 2 * LANES)), const((1, LANES)),
            const((tm, tm)), const((1, LANES)),
        ],
        out_specs=[row_tile(D_MODEL), row_tile(D_MODEL), row_tile(LANES), row_tile(LANES), row_tile(LANES),
                   const((1, LANES))],
        out_shape=[
            jax.ShapeDtypeStruct((T, D_MODEL), F32),
            jax.ShapeDtypeStruct((T, D_MODEL), F32),
            jax.ShapeDtypeStruct((T, LANES), jnp.int32),
            jax.ShapeDtypeStruct((T, LANES), F32),
            jax.ShapeDtypeStruct((T, LANES), jnp.int32),
            jax.ShapeDtypeStruct((1, LANES), F32),
        ],
        scratch_shapes=[pltpu.VMEM((2, tm, CONV_WIDTH), BF16), pltpu.VMEM((1, LANES), F32)],
        compiler_params=_params(("arbitrary",)),
        name="mixer_out_router",
    )(attn, gates3, gates3, gates3, gates3, gates3, gates3, gates3, x2d, w_out, conv_w,
      gc, g2, wr_hi, wr_both, br, tri, base_counts)


def _dispatch_body(nf_ref, slot_ref, ha_ref, hb_ref, xs_hbm, zero_scr, zsem, sem, *, steps_a):
    i = pl.program_id(0)
    n_sub = xs_hbm.shape[0] // MOE_SUB

    def zero_copy(sb):
        return pltpu.make_async_copy(zero_scr, xs_hbm.at[pl.ds(sb * MOE_SUB, MOE_SUB), :], zsem)

    @pl.when(i == 0)
    def _():
        zero_scr[...] = jnp.zeros_like(zero_scr)

        def start(sb, cnt):
            pl.when(nf_ref[sb] != 0)(lambda: zero_copy(sb).start())
            return cnt + nf_ref[sb]

        cnt = lax.fori_loop(0, n_sub, start, 0)

        def wait(_, c):
            zero_copy(0).wait()
            return c

        lax.fori_loop(0, cnt, wait, 0)

    def scatter(h_ref):
        for j in range(DISP_R):
            for k in range(TOP_K):
                s = slot_ref[0, j * TOP_K + k]
                pltpu.make_async_copy(h_ref.at[pl.ds(j, 1), :], xs_hbm.at[pl.ds(s, 1), :], sem).start(priority=k % 2)

    pl.when(i < steps_a)(lambda: scatter(ha_ref))
    pl.when(i >= steps_a)(lambda: scatter(hb_ref))
    n_rows = DISP_R * TOP_K
    pltpu.make_async_copy(xs_hbm.at[pl.ds(0, n_rows), :], xs_hbm.at[pl.ds(0, n_rows), :], sem).wait()


def _dispatch(h2_a, h2_b, slot, notfull, P):
    steps_a = h2_a.shape[0] // DISP_R
    steps_b = h2_b.shape[0] // DISP_R
    n_steps = steps_a + steps_b
    grid_spec = pltpu.PrefetchScalarGridSpec(
        num_scalar_prefetch=1,
        grid=(n_steps,),
        in_specs=[
            pl.BlockSpec((None, 1, DISP_R * TOP_K), lambda i, nf: (i, 0, 0), memory_space=pltpu.SMEM),
            pl.BlockSpec((DISP_R, D_MODEL), lambda i, nf: (jnp.minimum(i, steps_a - 1), 0)),
            pl.BlockSpec((DISP_R, D_MODEL), lambda i, nf: (jnp.maximum(i - steps_a, 0), 0)),
        ],
        out_specs=pl.BlockSpec(memory_space=pl.ANY),
        scratch_shapes=[pltpu.VMEM((MOE_SUB, D_MODEL), F32), pltpu.SemaphoreType.DMA(()),
                        pltpu.SemaphoreType.DMA(())],
    )
    return pl.pallas_call(
        functools.partial(_dispatch_body, steps_a=steps_a),
        grid_spec=grid_spec,
        out_shape=jax.ShapeDtypeStruct((P, D_MODEL), F32),
        compiler_params=_params(("arbitrary",)),
        name="dispatch",
    )(notfull, slot.reshape(n_steps, 1, DISP_R * TOP_K), h2_a, h2_b)


def _expert_body(te_ref, tx_ref, ns_ref, first_ref, x_ref, bg_ref, bu_ref, bd_ref, wg_hbm, wu_hbm, wd_hbm,
                 y_ref, xb, wg_res, wu_res, wd_res, stg_g, stg_u, stg_d, sems):
    i = pl.program_id(0)
    e = te_ref[i]
    nsub = ns_ref[i]
    first = first_ref[i]
    nf = D_FF // MOE_TF
    kh = D_MODEL // 2
    fh = MOE_TF // 2

    full_later_tile = (first == 0) & (nsub == MOE_NSUB)

    @pl.when(jnp.logical_not(full_later_tile))
    def _():
        for s in range(MOE_NSUB):
            rows = slice(s * MOE_SUB, (s + 1) * MOE_SUB)
            y_ref[rows, :] = jnp.where(s < nsub, jnp.broadcast_to(bd_ref[...], (MOE_SUB, D_MODEL)), 0.0)

    @pl.when((nsub > 0) & jnp.logical_not(full_later_tile))
    def _():
        xb[...] = x_ref[...].astype(BF16)

    def sub_block(s, f, from_bias=False, n_sub=1):
        rows = slice(s * MOE_SUB, (s + n_sub) * MOE_SUB)
        x = xb[rows, :]
        g = jnp.dot(x, wg_res[f], preferred_element_type=F32) + bg_ref[f]
        u = jnp.dot(x, wu_res[f], preferred_element_type=F32) + bu_ref[f]
        g = jnp.minimum(g, SWIGLU_LIMIT)
        u = jnp.clip(u, -SWIGLU_LIMIT, SWIGLU_LIMIT)
        a = (u + 1.0) * (g * jax.nn.sigmoid(SWIGLU_ALPHA * g))
        dn = jnp.dot(a.astype(BF16), wd_res[f], preferred_element_type=F32)
        if from_bias:
            y_ref[rows, :] = bd_ref[...] + dn
        else:
            y_ref[rows, :] += dn

    def half_copies(f, h):
        cols = pl.ds(f * MOE_TF, MOE_TF)
        drows = pl.ds(f * MOE_TF + h * fh, fh)
        return (pltpu.make_async_copy(wg_hbm.at[e, pl.ds(h * kh, kh), cols], stg_g.at[h], sems.at[0, h]),
                pltpu.make_async_copy(wu_hbm.at[e, pl.ds(h * kh, kh), cols], stg_u.at[h], sems.at[1, h]),
                pltpu.make_async_copy(wd_hbm.at[e, drows, :], stg_d.at[h], sems.at[2, h]))

    @pl.when(first == 1)
    def _():
        for h in range(2):
            for c in half_copies(0, h):
                c.start()

        def load_and_compute(f, carry):
            for h in range(2):
                for c in half_copies(f, h):
                    c.wait()
                wg_res[f, h * kh:(h + 1) * kh, :] = stg_g[h].astype(BF16)
                wu_res[f, h * kh:(h + 1) * kh, :] = stg_u[h].astype(BF16)
                wd_res[f, h * fh:(h + 1) * fh, :] = stg_d[h].astype(BF16)

                @pl.when(f + 1 < nf)
                def _():
                    for c in half_copies(f + 1, h):
                        c.start()
            pl.when(nsub == MOE_NSUB)(functools.partial(sub_block, 0, f, n_sub=MOE_NSUB))
            for s in range(MOE_NSUB - 1):
                pl.when((s < nsub) & (nsub < MOE_NSUB))(functools.partial(sub_block, s, f))
            return carry

        lax.fori_loop(0, nf, load_and_compute, 0)

    @pl.when(full_later_tile)
    def _():
        xb[...] = x_ref[...].astype(BF16)
        sub_block(0, 0, from_bias=True, n_sub=MOE_NSUB)

        def full(f, carry):
            sub_block(0, f, n_sub=MOE_NSUB)
            return carry
        lax.fori_loop(1, nf, full, 0, unroll=4)

    @pl.when((first == 0) & (nsub > 0) & (nsub < MOE_NSUB))
    def _():
        def part(f, carry):
            for s in range(MOE_NSUB - 1):
                pl.when(s < nsub)(functools.partial(sub_block, s, f))
            return carry
        lax.fori_loop(0, nf, part, 0)


def _expert_ffn(xs, tile_expert, tile_xblk, tile_nsub, tile_first, w_gate, b_gate, w_up, b_up, w_down, b_down):
    P = xs.shape[0]
    n_tiles = P // MOE_TILE
    nf = D_FF // MOE_TF
    per_expert = lambda shape: pl.BlockSpec(
        (None,) + shape, lambda i, te, tx, ns, fi: (te[i],) + (0,) * len(shape))
    grid_spec = pltpu.PrefetchScalarGridSpec(
        num_scalar_prefetch=4,
        grid=(n_tiles,),
        in_specs=[
            pl.BlockSpec((MOE_TILE, D_MODEL), lambda i, te, tx, ns, fi: (tx[i], 0)),
            per_expert((nf, 1, MOE_TF)), per_expert((nf, 1, MOE_TF)), per_expert((1, D_MODEL)),
            pl.BlockSpec(memory_space=pl.ANY), pl.BlockSpec(memory_space=pl.ANY), pl.BlockSpec(memory_space=pl.ANY),
        ],
        out_specs=pl.BlockSpec((MOE_TILE, D_MODEL), lambda i, te, tx, ns, fi: (i, 0)),
        scratch_shapes=[
            pltpu.VMEM((MOE_TILE, D_MODEL), BF16),
            pltpu.VMEM((nf, D_MODEL, MOE_TF), BF16), pltpu.VMEM((nf, D_MODEL, MOE_TF), BF16),
            pltpu.VMEM((nf, MOE_TF, D_MODEL), BF16),
            pltpu.VMEM((2, D_MODEL // 2, MOE_TF), F32), pltpu.VMEM((2, D_MODEL // 2, MOE_TF), F32),
            pltpu.VMEM((2, MOE_TF // 2, D_MODEL), F32),
            pltpu.SemaphoreType.DMA((3, 2)),
        ],
    )
    return pl.pallas_call(
        _expert_body,
        grid_spec=grid_spec,
        out_shape=jax.ShapeDtypeStruct((P, D_MODEL), F32),
        compiler_params=_params(("arbitrary",)),
        name="expert_ffn",
    )(tile_expert, tile_xblk, tile_nsub, tile_first, xs,
      b_gate.reshape(N_EXPERTS, nf, 1, MOE_TF), b_up.reshape(N_EXPERTS, nf, 1, MOE_TF),
      b_down.reshape(N_EXPERTS, 1, D_MODEL), w_gate, w_up, w_down)


def _final_body(slot_ref, slot_next_ref, x1_ref, gate_ref, gf_ref, y_hbm, o_ref, ybuf, sems, *, n):
    i = pl.program_id(0)
    tm = FIN_TM

    def gather(s_ref, b):
        for j in range(tm):
            for k in range(TOP_K):
                s = s_ref[0, j * TOP_K + k]
                pltpu.make_async_copy(y_hbm.at[pl.ds(s, 1), :], ybuf.at[b, k, pl.ds(j, 1), :],
                                      sems.at[b]).start(priority=k % 2)

    b = i % 2
    pl.when(i == 0)(lambda: gather(slot_ref, 0))
    pl.when((i + 1 < n) & (b == 0))(lambda: gather(slot_next_ref, 1))
    pl.when((i + 1 < n) & (b == 1))(lambda: gather(slot_next_ref, 0))
    pltpu.make_async_copy(ybuf.at[b], ybuf.at[b], sems.at[b]).wait()

    gates = gate_ref[...]
    acc = x1_ref[...]
    for k in range(TOP_K):
        acc = acc + gates[:, k:k + 1] * ybuf[b, k]
    ms = jnp.mean(acc * acc, axis=-1, keepdims=True)
    o_ref[...] = acc * lax.rsqrt(ms + EPS) * gf_ref[...]


def _combine_final(x1, gates, slot, y, gf):
    T = x1.shape[0]
    tm = FIN_TM
    n_steps = T // tm
    row_tile = lambda w: pl.BlockSpec((tm, w), lambda i: (i, 0))
    slot3 = slot.reshape(n_steps, 1, tm * TOP_K)
    slot_spec = lambda step: pl.BlockSpec((None, 1, tm * TOP_K), step, memory_space=pltpu.SMEM)
    return pl.pallas_call(
        functools.partial(_final_body, n=n_steps),
        grid=(n_steps,),
        in_specs=[
            slot_spec(lambda i: (i, 0, 0)),
            slot_spec(lambda i: (jnp.minimum(i + 1, n_steps - 1), 0, 0)),
            row_tile(D_MODEL), row_tile(LANES),
            pl.BlockSpec((1, D_MODEL), lambda i: (0, 0)),
            pl.BlockSpec(memory_space=pl.ANY),
        ],
        out_specs=row_tile(D_MODEL),
        out_shape=jax.ShapeDtypeStruct((T, D_MODEL), F32),
        scratch_shapes=[pltpu.VMEM((2, TOP_K, tm, D_MODEL), F32), pltpu.SemaphoreType.DMA((2,))],
        compiler_params=_params(("arbitrary",)),
        name="combine_final_norm",
    )(slot3, slot3, x1, gates, gf, y)


def _route(idx, rank, counts):
    n = idx.shape[0] * TOP_K
    n_tiles = n // MOE_TILE + N_EXPERTS
    tiles_per_e = (counts + MOE_TILE - 1) // MOE_TILE
    tile_end = jnp.cumsum(tiles_per_e)
    tile_start = tile_end - tiles_per_e
    experts = jnp.arange(N_EXPERTS, dtype=jnp.int32)
    region = jnp.sum(jnp.where(idx[..., None] == experts, tile_start * MOE_TILE, 0), axis=-1)
    slot = region + rank

    t = jnp.arange(n_tiles, dtype=jnp.int32)
    used = t < tile_end[-1]
    last_used = jnp.maximum(tile_end[-1] - 1, 0)
    tt = jnp.where(used, t, last_used)
    te = jnp.minimum(jnp.sum(tt[:, None] >= tile_end[None, :], axis=1), N_EXPERTS - 1).astype(jnp.int32)
    rows = jnp.where(used, jnp.clip(counts[te] - (tt - tile_start[te]) * MOE_TILE, 0, MOE_TILE), 0)
    nsub = ((rows + MOE_SUB - 1) // MOE_SUB).astype(jnp.int32)
    sub_end = (jnp.arange(MOE_NSUB, dtype=jnp.int32) + 1) * MOE_SUB
    notfull = (rows[:, None] < sub_end[None, :]).astype(jnp.int32).reshape(-1)
    first = (used & (t == tile_start[te])).astype(jnp.int32)
    return slot, notfull, te, tt.astype(jnp.int32), nsub, first, n_tiles * MOE_TILE


def kernel(x_prompt, x_sample, norm1_g, w_in, conv_w, attn_norm_g, conv_norm_g, w_out, norm2_g,
           w_router, b_router, w_gate, b_gate, w_up, b_up, w_down, b_down, final_norm_g):
    assert norm1_g.shape[0] == 1, "one layer"
    w_in_b = w_in[0].astype(BF16)
    w_out_b = w_out[0].astype(BF16)
    wr = jnp.pad(w_router[0], ((0, 0), (0, LANES - N_EXPERTS)))
    wr_hi = wr.astype(BF16)
    wr_both = jnp.concatenate([wr_hi, (wr - wr_hi.astype(F32)).astype(BF16)], axis=1)
    br = jnp.pad(b_router[0], (0, LANES - N_EXPERTS), constant_values=-jnp.inf).reshape(1, LANES)
    g1 = norm1_g[0].reshape(1, D_MODEL)
    g2 = norm2_g[0].reshape(1, D_MODEL)
    ga = attn_norm_g[0].reshape(1, ATTN_WIDTH)
    gc = conv_norm_g[0].reshape(1, CONV_WIDTH)
    gf = final_norm_g.reshape(1, D_MODEL)

    x1s, h2s, idxs, ranks, gates = [], [], [], [], []
    counts = jnp.zeros((1, LANES), F32)
    for x in (x_prompt, x_sample):
        B, S, _ = x.shape
        x2d = x.reshape(B * S, D_MODEL)
        qkv, gates3 = _in_projection(x2d, S, g1, w_in_b, _rope_tables(S))
        attn = _band_attention(qkv, B, S, ga)
        x1, h2, idx, gate, rank, counts = _mixer_out(
            x2d, S, attn, gates3, conv_w[0], gc, w_out_b, g2, wr_hi, wr_both, br, counts)
        x1s.append(x1)
        h2s.append(h2)
        idxs.append(idx[:, :TOP_K])
        ranks.append(rank[:, :TOP_K])
        gates.append(gate)

    slot, notfull, te, tx, nsub, first, P = _route(
        jnp.concatenate(idxs, axis=0), jnp.concatenate(ranks, axis=0), counts[0, :N_EXPERTS].astype(jnp.int32))
    slots = []
    t0 = 0
    for h2 in h2s:
        slots.append(slot[t0:t0 + h2.shape[0]])
        t0 += h2.shape[0]
    xs = _dispatch(h2s[0], h2s[1], slot, notfull, P)
    y = _expert_ffn(xs, te, tx, nsub, first, w_gate[0], b_gate[0], w_up[0], b_up[0], w_down[0], b_down[0])
    return tuple(_combine_final(x1, gate, sl, y, gf).reshape(x.shape)
                 for x, x1, gate, sl in zip((x_prompt, x_sample), x1s, gates, slots))
```

```python
import functools

import jax
import jax.numpy as jnp
from jax import lax
from jax.experimental import pallas as pl
from jax.experimental.pallas import tpu as pltpu

F32 = jnp.float32
BF16 = jnp.bfloat16

D_MODEL = 2048
HEAD_DIM = 128
N_HEADS = 8
ATTN_WIDTH = N_HEADS * HEAD_DIM
CONV_WIDTH = D_MODEL - ATTN_WIDTH
N_CONV_GROUPS = 8
QKV_WIDTH = 3 * ATTN_WIDTH
GATE_WIDTH = 3 * CONV_WIDTH
DILATED_PATTERNS = ((128, 1), (512, 4), (2048, 16))
ROPE_THETA = 500000.0
ROPE_DIM = HEAD_DIM // 4
ROPE_HALF = ROPE_DIM // 2
N_EXPERTS = 32
TOP_K = 4
D_FF = D_MODEL
SWIGLU_LIMIT = 7.0
SWIGLU_ALPHA = 1.702
EPS = 1e-5

LANES = 128
BF16_SUBLANES = 16
VMEM_LIMIT = 56 * 1024 * 1024

IN_TM = 1024
IN_TN = 1024
ATTN_QB = 128
ATTN_UNROLL = 16
ATTN_FIN = 256
MIX_TM = 512
DISP_R = 512
MOE_SUB = 256
MOE_TILE = 512
MOE_NSUB = MOE_TILE // MOE_SUB
MOE_TF = 256
FIN_TM = 256


def _params(sem):
    return pltpu.CompilerParams(dimension_semantics=sem, vmem_limit_bytes=VMEM_LIMIT)


def _normed(x_ref, g_ref, h_scr):
    x = x_ref[...]
    ms = jnp.mean(x * x, axis=-1, keepdims=True)
    h_scr[...] = (x * lax.rsqrt(ms + EPS) * g_ref[...]).astype(BF16)


def _qkv_body(x_ref, g_ref, w_ref, c_ref, sa_ref, sb_ref, o_ref, h_scr):
    pl.when(pl.program_id(1) == 0)(functools.partial(_normed, x_ref, g_ref, h_scr))
    acc = jnp.dot(h_scr[...], w_ref[...], preferred_element_type=F32)
    c, sa, sb = c_ref[...], sa_ref[...], sb_ref[...]
    for h in range(N_HEADS):
        sl = slice(h * HEAD_DIM, (h + 1) * HEAD_DIM)
        a = acc[:, sl]
        o_ref[:, sl] = a * c + pltpu.roll(a, HEAD_DIM - ROPE_HALF, 1) * sa + pltpu.roll(a, ROPE_HALF, 1) * sb


def _gates_body(x_ref, g_ref, w_ref, o_ref, h_scr):
    pl.when(pl.program_id(1) == 0)(functools.partial(_normed, x_ref, g_ref, h_scr))
    o_ref[...] = jnp.dot(h_scr[...], w_ref[...], preferred_element_type=F32).astype(BF16)


def _rope_tables(S):
    inv_freq = ROPE_THETA ** (-jnp.arange(ROPE_HALF, dtype=F32) * 2.0 / ROPE_DIM)
    ang = jnp.arange(S, dtype=F32)[:, None] * inv_freq[None, :]
    cos, sin = jnp.cos(ang), jnp.sin(ang)
    pad = jnp.zeros((S, HEAD_DIM - ROPE_DIM), F32)
    zero = jnp.zeros((S, ROPE_HALF), F32)
    c = jnp.concatenate([cos, cos, pad + 1.0], axis=1)
    sa = jnp.concatenate([-sin, zero, pad], axis=1)
    sb = jnp.concatenate([zero, sin, pad], axis=1)
    scale = HEAD_DIM ** -0.5
    none = jnp.zeros_like(c)
    return (jnp.stack([c * scale, c, none + 1.0]), jnp.stack([sa * scale, sa, none]),
            jnp.stack([sb * scale, sb, none]))


def _in_projection(x2d, S, gain, w_bf16, tables):
    T = x2d.shape[0]
    tm = IN_TM
    pos_blocks = S // tm
    n_qkv = QKV_WIDTH // IN_TN
    x_spec = pl.BlockSpec((tm, D_MODEL), lambda i, j: (i, 0))
    g_spec = pl.BlockSpec((1, D_MODEL), lambda i, j: (0, 0))
    out_spec = pl.BlockSpec((tm, IN_TN), lambda i, j: (i, j))
    tab_spec = pl.BlockSpec((None, tm, HEAD_DIM), lambda i, j: (j, i % pos_blocks, 0))
    common = dict(scratch_shapes=[pltpu.VMEM((tm, D_MODEL), BF16)],
                  compiler_params=_params(("arbitrary", "arbitrary")))
    qkv = pl.pallas_call(
        _qkv_body,
        grid=(T // tm, n_qkv),
        in_specs=[x_spec, g_spec, pl.BlockSpec((D_MODEL, IN_TN), lambda i, j: (0, j)), tab_spec, tab_spec, tab_spec],
        out_specs=out_spec,
        out_shape=jax.ShapeDtypeStruct((T, QKV_WIDTH), F32),
        name="qkv_projection", **common,
    )(x2d, gain, w_bf16, *tables)
    gates = pl.pallas_call(
        _gates_body,
        grid=(T // tm, GATE_WIDTH // IN_TN),
        in_specs=[x_spec, g_spec, pl.BlockSpec((D_MODEL, IN_TN), lambda i, j: (0, n_qkv + j))],
        out_specs=out_spec,
        out_shape=jax.ShapeDtypeStruct((T, GATE_WIDTH), BF16),
        name="gate_projection", **common,
    )(x2d, gain, w_bf16)
    return qkv, gates


def _attn_configs(S):
    cfgs, tables = [], []
    for window, d in DILATED_PATTERNS:
        L = S // d
        qb = min(ATTN_QB, L)
        half = window // (2 * d)
        W = min(qb + 2 * half, L)
        assert L % qb == 0 and window % (2 * d) == 0
        assert (W == qb + 2 * half and L // qb >= 2) or (L == qb == W)
        if (qb, W, half) not in tables:
            tables.append((qb, W, half))
        cfgs.append((d, L, qb, W, half, tables.index((qb, W, half))))
    return cfgs, tables


def _mask_table(qb, W, half):
    rel = jnp.arange(W, dtype=jnp.int32)[None, :] - jnp.arange(qb, dtype=jnp.int32)[:, None]
    offs = jnp.array([half, 0, -half], jnp.int32)[:, None, None]
    u = rel[None] + offs
    return jnp.where((u >= 0) & (u <= 2 * half), 0.0, -1e30).astype(F32)


def _attn_body(*refs, S):
    cfgs, tables = _attn_configs(S)
    q_ref, k_ref, v_ref, gain_ref = refs[:4]
    mask_refs = refs[4:4 + len(tables)]
    o_ref, qc, kc, vc, og, lg, sc, mc, q32, k32, v32 = refs[4 + len(tables):]
    srcs, stages, f32s = (q_ref, k_ref, v_ref), (qc, kc, vc), (q32, k32, v32)
    d_prev = 1
    for g, (d, L, qb, W, half, tab) in enumerate(cfgs):
        n_blocks = L // qb
        mask_ref = mask_refs[tab]

        def rows_of_class(r, start, size, d=d):
            if d == 1:
                return pl.ds(start, size)
            return pl.ds(r + start * d, size, stride=d)

        from_prev = d_prev > 1 and d % d_prev == 0
        keep_f32 = any(c[0] > d and c[0] % d == 0 for c in cfgs) and d > 1
        step = d // d_prev if from_prev else d

        def stage(r, carry, d=d, L=L, qb=qb, from_prev=from_prev, keep_f32=keep_f32, step=step, d_prev=d_prev):
            dst = pl.ds(pl.multiple_of(r * L, qb), L)
            for src, st, f32 in zip(srcs, stages, f32s):
                if from_prev:
                    start = (r % d_prev) * (L * step) + r // d_prev
                    x = f32[pl.ds(start, L, stride=step), :]
                elif d == 1:
                    x = src[pl.ds(0, L), :]
                else:
                    x = src[pl.ds(r, L, stride=d), :]
                if keep_f32:
                    f32[dst, :] = x
                st[dst, :] = x.astype(BF16)
            return carry

        if d == 1:
            stage(0, 0)
        else:
            lax.fori_loop(0, d, stage, 0)
        d_prev = d

        def place(nb, L=L, qb=qb, half=half, W=W, n_blocks=n_blocks):
            r = nb // n_blocks
            q0 = pl.multiple_of((nb % n_blocks) * qb, qb)
            ws = jnp.clip(q0 - half, 0, L - W)
            case = jnp.where(q0 == 0, 0, jnp.where(q0 == L - qb, 2, 1))
            keys = pl.ds(pl.multiple_of(r * L + ws, BF16_SUBLANES), W)
            return r, q0, keys, case

        def scores(nb, c, g=g, L=L, qb=qb, W=W, mask_ref=mask_ref, place=place, rows_of_class=rows_of_class):
            r, q0, keys, case = place(nb)
            q = qc[pl.ds(pl.multiple_of(r * L + q0, qb), qb), :]
            s = lax.dot_general(q, kc[keys, :], (((1,), (1,)), ((), ())), preferred_element_type=F32)
            s = s + mask_ref[case]
            m = jnp.max(s, axis=-1, keepdims=True)
            sc[nb, 0:qb, 0:W] = s - m
            mc[pl.ds(pl.multiple_of(nb * qb, qb), qb), :] = jnp.broadcast_to(m, (qb, LANES))
            return c

        def values(nb, c, g=g, qb=qb, W=W, place=place, rows_of_class=rows_of_class):
            r, q0, keys, _ = place(nb)
            p = jnp.exp(sc[nb, 0:qb, 0:W])
            l = jnp.sum(p, axis=-1, keepdims=True)
            acc = jnp.dot(p.astype(BF16), vc[keys, :], preferred_element_type=F32)
            og[g, rows_of_class(r, q0, qb), :] = acc / l
            lg[g, rows_of_class(r, q0, qb), :] = mc[pl.ds(pl.multiple_of(nb * qb, qb), qb), :] + jnp.log(l)
            return c

        unroll = min(ATTN_UNROLL, S // qb)
        lax.fori_loop(0, S // qb, scores, 0, unroll=unroll)
        lax.fori_loop(0, S // qb, values, 0, unroll=unroll)

    def combine(c, carry):
        rows = pl.ds(pl.multiple_of(c * ATTN_FIN, ATTN_FIN), ATTN_FIN)
        l0, l1, l2 = lg[0, rows, :], lg[1, rows, :], lg[2, rows, :]
        mx = jnp.maximum(jnp.maximum(l0, l1), l2)
        e0, e1, e2 = jnp.exp(l0 - mx), jnp.exp(l1 - mx), jnp.exp(l2 - mx)
        a = (e0 * og[0, rows, :] + e1 * og[1, rows, :] + e2 * og[2, rows, :]) / (e0 + e1 + e2)
        ms = jnp.mean(a * a, axis=-1, keepdims=True)
        o_ref[rows, :] = (a * lax.rsqrt(ms + EPS) * gain_ref[...]).astype(BF16)
        return carry

    lax.fori_loop(0, S // ATTN_FIN, combine, 0, unroll=4)


def _band_attention(qkv, B, S, gain):
    cfgs, tables = _attn_configs(S)
    score_shape = (max(S // c[2] for c in cfgs), max(c[2] for c in cfgs), max(c[3] for c in cfgs))
    head = lambda t: pl.BlockSpec((S, HEAD_DIM), lambda b, h: (b, t * N_HEADS + h))
    masks = [_mask_table(*t) for t in tables]
    mask_specs = [pl.BlockSpec(m.shape, lambda b, h: (0, 0, 0)) for m in masks]
    return pl.pallas_call(
        functools.partial(_attn_body, S=S),
        grid=(B, N_HEADS),
        in_specs=[head(0), head(1), head(2), pl.BlockSpec((1, HEAD_DIM), lambda b, h: (0, h))] + mask_specs,
        out_specs=pl.BlockSpec((S, HEAD_DIM), lambda b, h: (b, h)),
        out_shape=jax.ShapeDtypeStruct((B * S, ATTN_WIDTH), BF16),
        scratch_shapes=[pltpu.VMEM((S, HEAD_DIM), BF16)] * 3
        + [pltpu.VMEM((len(DILATED_PATTERNS), S, HEAD_DIM), F32)] * 2
        + [pltpu.VMEM(score_shape, F32)] + [pltpu.VMEM((S, HEAD_DIM), F32)] * 4,
        compiler_params=_params(("arbitrary", "arbitrary")),
        name="band_attention",
    )(qkv, qkv, qkv, gain, *masks)


def _mix_body(attn_ref, cb_ref, cc_ref, cx_ref, ccp_ref, cxp_ref, ccn_ref, cxn_ref, x_ref, wout_ref,
              convw_ref, gc_ref, g2_ref, wrh_ref, wrb_ref, br_ref, tri_ref, base_ref,
              x1_ref, h2_ref, idx_ref, gate_ref, rank_ref, cnt_ref, conv_scr, cnt_scr, *, tm, S, n):
    s = pl.program_id(0)
    live = s >= 1
    buf_a = s % 2
    buf_b = 1 - buf_a

    @pl.when(s == 0)
    def _():
        conv_scr[1] = jnp.zeros((tm, CONV_WIDTH), BF16)
        cnt_scr[...] = base_ref[...]

    mix = (jnp.dot(attn_ref[...], wout_ref[0:ATTN_WIDTH, :], preferred_element_type=F32)
           + jnp.dot(conv_scr[buf_b], wout_ref[ATTN_WIDTH:D_MODEL, :], preferred_element_type=F32))
    x1 = x_ref[...] + mix
    x1_ref[...] = x1
    ms = jnp.mean(x1 * x1, axis=-1, keepdims=True)
    h2 = x1 * lax.rsqrt(ms + EPS) * g2_ref[...]
    hi = h2.astype(BF16)
    hi_f = hi.astype(F32)
    h2_ref[...] = hi_f

    lo = (h2 - hi_f).astype(BF16)
    hi_both = jnp.dot(hi, wrb_ref[...], preferred_element_type=F32)
    logits = (hi_both[:, :LANES] + hi_both[:, LANES:]
              + jnp.dot(lo, wrh_ref[...], preferred_element_type=F32)) + br_ref[...]

    lane = lax.broadcasted_iota(jnp.int32, (tm, LANES), 1)
    lane_f = lane.astype(F32)
    vals, ids = [], []
    for _ in range(TOP_K):
        m = jnp.max(logits, axis=-1, keepdims=True)
        ix = jnp.min(jnp.where(logits == m, lane_f, float(LANES)), axis=-1, keepdims=True)
        vals.append(m)
        ids.append(ix)
        logits = jnp.where(lane_f == ix, -jnp.inf, logits)
    es = [jnp.exp(v - vals[0]) for v in vals]
    den = es[0] + es[1] + es[2] + es[3]
    idx_tile = jnp.zeros((tm, LANES), F32)
    gate_tile = jnp.zeros((tm, LANES), F32)
    for k in range(TOP_K):
        idx_tile = jnp.where(lane == k, ids[k], idx_tile)
        gate_tile = jnp.where(lane == k, es[k] / den, gate_tile)
    idx_ref[...] = idx_tile.astype(jnp.int32)
    gate_ref[...] = gate_tile

    chosen = [lane_f == ids[k] for k in range(TOP_K)]
    picks = sum(c.astype(F32) for c in chosen) * live.astype(F32)
    before = cnt_scr[...] + jnp.dot(tri_ref[...], picks.astype(BF16), preferred_element_type=F32)
    rank_tile = jnp.zeros((tm, LANES), F32)
    for k in range(TOP_K):
        rk = jnp.sum(jnp.where(chosen[k], before, 0.0), axis=-1, keepdims=True)
        rank_tile = jnp.where(lane == k, rk, rank_tile)
    rank_ref[...] = rank_tile.astype(jnp.int32)
    cnt_scr[...] += jnp.sum(picks, axis=0, keepdims=True)
    cnt_ref[...] = cnt_scr[...]

    row = lax.broadcasted_iota(jnp.int32, (tm, LANES), 0)
    r0 = jnp.minimum(s, n - 1) * tm
    at_start = (r0 % S) == 0
    at_end = ((r0 + tm) % S) == 0
    last = BF16_SUBLANES - 1
    gw = CONV_WIDTH // N_CONV_GROUPS
    for c in range(N_CONV_GROUPS):
        sl = slice(c * gw, (c + 1) * gw)
        u = cc_ref[:, sl].astype(F32) * cx_ref[:, sl].astype(F32)
        up = ccp_ref[last:last + 1, sl].astype(F32) * cxp_ref[last:last + 1, sl].astype(F32)
        un = ccn_ref[0:1, sl].astype(F32) * cxn_ref[0:1, sl].astype(F32)
        up = jnp.where(at_start, 0.0, up)
        un = jnp.where(at_end, 0.0, un)
        u_prev = jnp.where(row == 0, up, pltpu.roll(u, 1, 0))
        u_next = jnp.where(row == tm - 1, un, pltpu.roll(u, tm - 1, 0))
        y = convw_ref[0:1, sl] * u_prev + convw_ref[1:2, sl] * u + convw_ref[2:3, sl] * u_next
        cv = cb_ref[:, sl].astype(F32) * y
        ms = jnp.mean(cv * cv, axis=-1, keepdims=True)
        conv_scr[buf_a, :, sl] = (cv * lax.rsqrt(ms + EPS) * gc_ref[:, sl]).astype(BF16)


def _mixer_out(x2d, S, attn, gates3, conv_w, gc, w_out, g2, wr_hi, wr_both, br, base_counts):
    T = x2d.shape[0]
    tm = MIX_TM
    tri = (jnp.arange(tm)[:, None] > jnp.arange(tm)[None, :]).astype(BF16)
    hr = BF16_SUBLANES
    n_halo = T // hr
    n = T // tm
    done = lambda s: jnp.maximum(s - 1, 0)
    conv = lambda s: jnp.minimum(s, n - 1)
    row_tile = lambda w: pl.BlockSpec((tm, w), lambda s: (done(s), 0))
    gate_tile = lambda c: pl.BlockSpec((tm, CONV_WIDTH), lambda s: (conv(s), c))
    prev_halo = lambda c: pl.BlockSpec(
        (hr, CONV_WIDTH), lambda s: (jnp.maximum(conv(s) * (tm // hr) - 1, 0), c))
    next_halo = lambda c: pl.BlockSpec(
        (hr, CONV_WIDTH), lambda s: (jnp.minimum((conv(s) + 1) * (tm // hr), n_halo - 1), c))
    const = lambda shape: pl.BlockSpec(shape, lambda s: (0, 0))
    cb_col, cc_col, cx_col = 0, 1, 2
    return pl.pallas_call(
        functools.partial(_mix_body, tm=tm, S=S, n=n),
        grid=(n + 1,),
        in_specs=[
            row_tile(ATTN_WIDTH),
            gate_tile(cb_col), gate_tile(cc_col), gate_tile(cx_col),
            prev_halo(cc_col), prev_halo(cx_col), next_halo(cc_col), next_halo(cx_col),
            row_tile(D_MODEL),
            const((D_MODEL, D_MODEL)),
            const((3, CONV_WIDTH)), const((1, CONV_WIDTH)), const((1, D_MODEL)),
            const((D_MODEL, LANES)), const((D_MODEL, 2 * LANES)), const((1, LANES)),
            const((tm, tm)), const((1, LANES)),
        ],
        out_specs=[row_tile(D_MODEL), row_tile(D_MODEL), row_tile(LANES), row_tile(LANES), row_tile(LANES),
                   const((1, LANES))],
        out_shape=[
            jax.ShapeDtypeStruct((T, D_MODEL), F32),
            jax.ShapeDtypeStruct((T, D_MODEL), F32),
            jax.ShapeDtypeStruct((T, LANES), jnp.int32),
            jax.ShapeDtypeStruct((T, LANES), F32),
            jax.ShapeDtypeStruct((T, LANES), jnp.int32),
            jax.ShapeDtypeStruct((1, LANES), F32),
        ],
        scratch_shapes=[pltpu.VMEM((2, tm, CONV_WIDTH), BF16), pltpu.VMEM((1, LANES), F32)],
        compiler_params=_params(("arbitrary",)),
        name="mixer_out_router",
    )(attn, gates3, gates3, gates3, gates3, gates3, gates3, gates3, x2d, w_out, conv_w,
      gc, g2, wr_hi, wr_both, br, tri, base_counts)


def _dispatch_body(nf_ref, slot_ref, ha_ref, hb_ref, xs_hbm, zero_scr, zsem, sem, *, steps_a):
    i = pl.program_id(0)
    n_sub = xs_hbm.shape[0] // MOE_SUB

    def zero_copy(sb):
        return pltpu.make_async_copy(zero_scr, xs_hbm.at[pl.ds(sb * MOE_SUB, MOE_SUB), :], zsem)

    @pl.when(i == 0)
    def _():
        zero_scr[...] = jnp.zeros_like(zero_scr)

        def start(sb, cnt):
            pl.when(nf_ref[sb] != 0)(lambda: zero_copy(sb).start())
            return cnt + nf_ref[sb]

        cnt = lax.fori_loop(0, n_sub, start, 0)

        def wait(_, c):
            zero_copy(0).wait()
            return c

        lax.fori_loop(0, cnt, wait, 0)

    def scatter(h_ref):
        for j in range(DISP_R):
            for k in range(TOP_K):
                s = slot_ref[0, j * TOP_K + k]
                pltpu.make_async_copy(h_ref.at[pl.ds(j, 1), :], xs_hbm.at[pl.ds(s, 1), :], sem).start(priority=k % 2)

    pl.when(i < steps_a)(lambda: scatter(ha_ref))
    pl.when(i >= steps_a)(lambda: scatter(hb_ref))
    n_rows = DISP_R * TOP_K
    pltpu.make_async_copy(xs_hbm.at[pl.ds(0, n_rows), :], xs_hbm.at[pl.ds(0, n_rows), :], sem).wait()


def _dispatch(h2_a, h2_b, slot, notfull, P):
    steps_a = h2_a.shape[0] // DISP_R
    steps_b = h2_b.shape[0] // DISP_R
    n_steps = steps_a + steps_b
    grid_spec = pltpu.PrefetchScalarGridSpec(
        num_scalar_prefetch=1,
        grid=(n_steps,),
        in_specs=[
            pl.BlockSpec((None, 1, DISP_R * TOP_K), lambda i, nf: (i, 0, 0), memory_space=pltpu.SMEM),
            pl.BlockSpec((DISP_R, D_MODEL), lambda i, nf: (jnp.minimum(i, steps_a - 1), 0)),
            pl.BlockSpec((DISP_R, D_MODEL), lambda i, nf: (jnp.maximum(i - steps_a, 0), 0)),
        ],
        out_specs=pl.BlockSpec(memory_space=pl.ANY),
        scratch_shapes=[pltpu.VMEM((MOE_SUB, D_MODEL), F32), pltpu.SemaphoreType.DMA(()),
                        pltpu.SemaphoreType.DMA(())],
    )
    return pl.pallas_call(
        functools.partial(_dispatch_body, steps_a=steps_a),
        grid_spec=grid_spec,
        out_shape=jax.ShapeDtypeStruct((P, D_MODEL), F32),
        compiler_params=_params(("arbitrary",)),
        name="dispatch",
    )(notfull, slot.reshape(n_steps, 1, DISP_R * TOP_K), h2_a, h2_b)


def _expert_body(te_ref, tx_ref, ns_ref, first_ref, x_ref, bg_ref, bu_ref, bd_ref, wg_hbm, wu_hbm, wd_hbm,
                 y_ref, xb, wg_res, wu_res, wd_res, stg_g, stg_u, stg_d, sems):
    i = pl.program_id(0)
    e = te_ref[i]
    nsub = ns_ref[i]
    first = first_ref[i]
    nf = D_FF // MOE_TF
    kh = D_MODEL // 2
    fh = MOE_TF // 2

    full_later_tile = (first == 0) & (nsub == MOE_NSUB)

    @pl.when(jnp.logical_not(full_later_tile))
    def _():
        for s in range(MOE_NSUB):
            rows = slice(s * MOE_SUB, (s + 1) * MOE_SUB)
            y_ref[rows, :] = jnp.where(s < nsub, jnp.broadcast_to(bd_ref[...], (MOE_SUB, D_MODEL)), 0.0)

    @pl.when((nsub > 0) & jnp.logical_not(full_later_tile))
    def _():
        xb[...] = x_ref[...].astype(BF16)

    def sub_block(s, f, from_bias=False, n_sub=1):
        rows = slice(s * MOE_SUB, (s + n_sub) * MOE_SUB)
        x = xb[rows, :]
        g = jnp.dot(x, wg_res[f], preferred_element_type=F32) + bg_ref[f]
        u = jnp.dot(x, wu_res[f], preferred_element_type=F32) + bu_ref[f]
        g = jnp.minimum(g, SWIGLU_LIMIT)
        u = jnp.clip(u, -SWIGLU_LIMIT, SWIGLU_LIMIT)
        a = (u + 1.0) * (g * jax.nn.sigmoid(SWIGLU_ALPHA * g))
        dn = jnp.dot(a.astype(BF16), wd_res[f], preferred_element_type=F32)
        if from_bias:
            y_ref[rows, :] = bd_ref[...] + dn
        else:
            y_ref[rows, :] += dn

    def half_copies(f, h):
        cols = pl.ds(f * MOE_TF, MOE_TF)
        drows = pl.ds(f * MOE_TF + h * fh, fh)
        return (pltpu.make_async_copy(wg_hbm.at[e, pl.ds(h * kh, kh), cols], stg_g.at[h], sems.at[0, h]),
                pltpu.make_async_copy(wu_hbm.at[e, pl.ds(h * kh, kh), cols], stg_u.at[h], sems.at[1, h]),
                pltpu.make_async_copy(wd_hbm.at[e, drows, :], stg_d.at[h], sems.at[2, h]))

    @pl.when(first == 1)
    def _():
        for h in range(2):
            for c in half_copies(0, h):
                c.start()

        def load_and_compute(f, carry):
            for h in range(2):
                for c in half_copies(f, h):
                    c.wait()
                wg_res[f, h * kh:(h + 1) * kh, :] = stg_g[h].astype(BF16)
                wu_res[f, h * kh:(h + 1) * kh, :] = stg_u[h].astype(BF16)
                wd_res[f, h * fh:(h + 1) * fh, :] = stg_d[h].astype(BF16)

                @pl.when(f + 1 < nf)
                def _():
                    for c in half_copies(f + 1, h):
                        c.start()
            pl.when(nsub == MOE_NSUB)(functools.partial(sub_block, 0, f, n_sub=MOE_NSUB))
            for s in range(MOE_NSUB - 1):
                pl.when((s < nsub) & (nsub < MOE_NSUB))(functools.partial(sub_block, s, f))
            return carry

        lax.fori_loop(0, nf, load_and_compute, 0)

    @pl.when(full_later_tile)
    def _():
        xb[...] = x_ref[...].astype(BF16)
        sub_block(0, 0, from_bias=True, n_sub=MOE_NSUB)

        def full(f, carry):
            sub_block(0, f, n_sub=MOE_NSUB)
            return carry
        lax.fori_loop(1, nf, full, 0, unroll=4)

    @pl.when((first == 0) & (nsub > 0) & (nsub < MOE_NSUB))
    def _():
        def part(f, carry):
            for s in range(MOE_NSUB - 1):
                pl.when(s < nsub)(functools.partial(sub_block, s, f))
            return carry
        lax.fori_loop(0, nf, part, 0)


def _expert_ffn(xs, tile_expert, tile_xblk, tile_nsub, tile_first, w_gate, b_gate, w_up, b_up, w_down, b_down):
    P = xs.shape[0]
    n_tiles = P // MOE_TILE
    nf = D_FF // MOE_TF
    per_expert = lambda shape: pl.BlockSpec(
        (None,) + shape, lambda i, te, tx, ns, fi: (te[i],) + (0,) * len(shape))
    grid_spec = pltpu.PrefetchScalarGridSpec(
        num_scalar_prefetch=4,
        grid=(n_tiles,),
        in_specs=[
            pl.BlockSpec((MOE_TILE, D_MODEL), lambda i, te, tx, ns, fi: (tx[i], 0)),
            per_expert((nf, 1, MOE_TF)), per_expert((nf, 1, MOE_TF)), per_expert((1, D_MODEL)),
            pl.BlockSpec(memory_space=pl.ANY), pl.BlockSpec(memory_space=pl.ANY), pl.BlockSpec(memory_space=pl.ANY),
        ],
        out_specs=pl.BlockSpec((MOE_TILE, D_MODEL), lambda i, te, tx, ns, fi: (i, 0)),
        scratch_shapes=[
            pltpu.VMEM((MOE_TILE, D_MODEL), BF16),
            pltpu.VMEM((nf, D_MODEL, MOE_TF), BF16), pltpu.VMEM((nf, D_MODEL, MOE_TF), BF16),
            pltpu.VMEM((nf, MOE_TF, D_MODEL), BF16),
            pltpu.VMEM((2, D_MODEL // 2, MOE_TF), F32), pltpu.VMEM((2, D_MODEL // 2, MOE_TF), F32),
            pltpu.VMEM((2, MOE_TF // 2, D_MODEL), F32),
            pltpu.SemaphoreType.DMA((3, 2)),
        ],
    )
    return pl.pallas_call(
        _expert_body,
        grid_spec=grid_spec,
        out_shape=jax.ShapeDtypeStruct((P, D_MODEL), F32),
        compiler_params=_params(("arbitrary",)),
        name="expert_ffn",
    )(tile_expert, tile_xblk, tile_nsub, tile_first, xs,
      b_gate.reshape(N_EXPERTS, nf, 1, MOE_TF), b_up.reshape(N_EXPERTS, nf, 1, MOE_TF),
      b_down.reshape(N_EXPERTS, 1, D_MODEL), w_gate, w_up, w_down)


def _final_body(slot_ref, slot_next_ref, x1_ref, gate_ref, gf_ref, y_hbm, o_ref, ybuf, sems, *, n):
    i = pl.program_id(0)
    tm = FIN_TM

    def gather(s_ref, b):
        for j in range(tm):
            for k in range(TOP_K):
                s = s_ref[0, j * TOP_K + k]
                pltpu.make_async_copy(y_hbm.at[pl.ds(s, 1), :], ybuf.at[b, k, pl.ds(j, 1), :],
                                      sems.at[b]).start(priority=k % 2)

    b = i % 2
    pl.when(i == 0)(lambda: gather(slot_ref, 0))
    pl.when((i + 1 < n) & (b == 0))(lambda: gather(slot_next_ref, 1))
    pl.when((i + 1 < n) & (b == 1))(lambda: gather(slot_next_ref, 0))
    pltpu.make_async_copy(ybuf.at[b], ybuf.at[b], sems.at[b]).wait()

    gates = gate_ref[...]
    acc = x1_ref[...]
    for k in range(TOP_K):
        acc = acc + gates[:, k:k + 1] * ybuf[b, k]
    ms = jnp.mean(acc * acc, axis=-1, keepdims=True)
    o_ref[...] = acc * lax.rsqrt(ms + EPS) * gf_ref[...]


def _combine_final(x1, gates, slot, y, gf):
    T = x1.shape[0]
    tm = FIN_TM
    n_steps = T // tm
    row_tile = lambda w: pl.BlockSpec((tm, w), lambda i: (i, 0))
    slot3 = slot.reshape(n_steps, 1, tm * TOP_K)
    slot_spec = lambda step: pl.BlockSpec((None, 1, tm * TOP_K), step, memory_space=pltpu.SMEM)
    return pl.pallas_call(
        functools.partial(_final_body, n=n_steps),
        grid=(n_steps,),
        in_specs=[
            slot_spec(lambda i: (i, 0, 0)),
            slot_spec(lambda i: (jnp.minimum(i + 1, n_steps - 1), 0, 0)),
            row_tile(D_MODEL), row_tile(LANES),
            pl.BlockSpec((1, D_MODEL), lambda i: (0, 0)),
            pl.BlockSpec(memory_space=pl.ANY),
        ],
        out_specs=row_tile(D_MODEL),
        out_shape=jax.ShapeDtypeStruct((T, D_MODEL), F32),
        scratch_shapes=[pltpu.VMEM((2, TOP_K, tm, D_MODEL), F32), pltpu.SemaphoreType.DMA((2,))],
        compiler_params=_params(("arbitrary",)),
        name="combine_final_norm",
    )(slot3, slot3, x1, gates, gf, y)


def _route(idx, rank, counts):
    n = idx.shape[0] * TOP_K
    n_tiles = n // MOE_TILE + N_EXPERTS
    tiles_per_e = (counts + MOE_TILE - 1) // MOE_TILE
    tile_end = jnp.cumsum(tiles_per_e)
    tile_start = tile_end - tiles_per_e
    experts = jnp.arange(N_EXPERTS, dtype=jnp.int32)
    region = jnp.sum(jnp.where(idx[..., None] == experts, tile_start * MOE_TILE, 0), axis=-1)
    slot = region + rank

    t = jnp.arange(n_tiles, dtype=jnp.int32)
    used = t < tile_end[-1]
    last_used = jnp.maximum(tile_end[-1] - 1, 0)
    tt = jnp.where(used, t, last_used)
    te = jnp.minimum(jnp.sum(tt[:, None] >= tile_end[None, :], axis=1), N_EXPERTS - 1).astype(jnp.int32)
    rows = jnp.where(used, jnp.clip(counts[te] - (tt - tile_start[te]) * MOE_TILE, 0, MOE_TILE), 0)
    nsub = ((rows + MOE_SUB - 1) // MOE_SUB).astype(jnp.int32)
    sub_end = (jnp.arange(MOE_NSUB, dtype=jnp.int32) + 1) * MOE_SUB
    notfull = (rows[:, None] < sub_end[None, :]).astype(jnp.int32).reshape(-1)
    first = (used & (t == tile_start[te])).astype(jnp.int32)
    return slot, notfull, te, tt.astype(jnp.int32), nsub, first, n_tiles * MOE_TILE


def kernel(x_prompt, x_sample, norm1_g, w_in, conv_w, attn_norm_g, conv_norm_g, w_out, norm2_g,
           w_router, b_router, w_gate, b_gate, w_up, b_up, w_down, b_down, final_norm_g):
    assert norm1_g.shape[0] == 1, "one layer"
    w_in_b = w_in[0].astype(BF16)
    w_out_b = w_out[0].astype(BF16)
    wr = jnp.pad(w_router[0], ((0, 0), (0, LANES - N_EXPERTS)))
    wr_hi = wr.astype(BF16)
    wr_both = jnp.concatenate([wr_hi, (wr - wr_hi.astype(F32)).astype(BF16)], axis=1)
    br = jnp.pad(b_router[0], (0, LANES - N_EXPERTS), constant_values=-jnp.inf).reshape(1, LANES)
    g1 = norm1_g[0].reshape(1, D_MODEL)
    g2 = norm2_g[0].reshape(1, D_MODEL)
    ga = attn_norm_g[0].reshape(1, ATTN_WIDTH)
    gc = conv_norm_g[0].reshape(1, CONV_WIDTH)
    gf = final_norm_g.reshape(1, D_MODEL)

    x1s, h2s, idxs, ranks, gates = [], [], [], [], []
    counts = jnp.zeros((1, LANES), F32)
    for x in (x_prompt, x_sample):
        B, S, _ = x.shape
        x2d = x.reshape(B * S, D_MODEL)
        qkv, gates3 = _in_projection(x2d, S, g1, w_in_b, _rope_tables(S))
        attn = _band_attention(qkv, B, S, ga)
        x1, h2, idx, gate, rank, counts = _mixer_out(
            x2d, S, attn, gates3, conv_w[0], gc, w_out_b, g2, wr_hi, wr_both, br, counts)
        x1s.append(x1)
        h2s.append(h2)
        idxs.append(idx[:, :TOP_K])
        ranks.append(rank[:, :TOP_K])
        gates.append(gate)

    slot, notfull, te, tx, nsub, first, P = _route(
        jnp.concatenate(idxs, axis=0), jnp.concatenate(ranks, axis=0), counts[0, :N_EXPERTS].astype(jnp.int32))
    slots = []
    t0 = 0
    for h2 in h2s:
        slots.append(slot[t0:t0 + h2.shape[0]])
        t0 += h2.shape[0]
    xs = _dispatch(h2s[0], h2s[1], slot, notfull, P)
    y = _expert_ffn(xs, te, tx, nsub, first, w_gate[0], b_gate[0], w_up[0], b_up[0], w_down[0], b_down[0])
    return tuple(_combine_final(x1, gate, sl, y, gf).reshape(x.shape)
                 for x, x1, gate, sl in zip((x_prompt, x_sample), x1s, gates, slots))
```

```python
import functools

import jax
import jax.numpy as jnp
from jax import lax
from jax.experimental import pallas as pl
from jax.experimental.pallas import tpu as pltpu

F32 = jnp.float32
BF16 = jnp.bfloat16

D_MODEL = 2048
HEAD_DIM = 128
N_HEADS = 8
ATTN_WIDTH = N_HEADS * HEAD_DIM
CONV_WIDTH = D_MODEL - ATTN_WIDTH
N_CONV_GROUPS = 8
QKV_WIDTH = 3 * ATTN_WIDTH
GATE_WIDTH = 3 * CONV_WIDTH
DILATED_PATTERNS = ((128, 1), (512, 4), (2048, 16))
ROPE_THETA = 500000.0
ROPE_DIM = HEAD_DIM // 4
ROPE_HALF = ROPE_DIM // 2
N_EXPERTS = 32
TOP_K = 4
D_FF = D_MODEL
SWIGLU_LIMIT = 7.0
SWIGLU_ALPHA = 1.702
EPS = 1e-5

LANES = 128
BF16_SUBLANES = 16
VMEM_LIMIT = 56 * 1024 * 1024

IN_TM = 1024
IN_TN = 1024
ATTN_QB = 128
ATTN_UNROLL = 16
ATTN_FIN = 256
MIX_TM = 512
DISP_R = 512
MOE_SUB = 256
MOE_TILE = 512
MOE_NSUB = MOE_TILE // MOE_SUB
MOE_TF = 256
FIN_TM = 256


def _params(sem):
    return pltpu.CompilerParams(dimension_semantics=sem, vmem_limit_bytes=VMEM_LIMIT)


def _qkv_body(x_ref, g_ref, w_ref, c_ref, sa_ref, sb_ref, o_ref, h_ref):
    @pl.when(pl.program_id(1) == 0)
    def _():
        x = x_ref[...]
        ms = jnp.mean(x * x, axis=-1, keepdims=True)
        h_ref[...] = (x * lax.rsqrt(ms + EPS) * g_ref[...]).astype(BF16)

    acc = jnp.dot(h_ref[...], w_ref[...], preferred_element_type=F32)
    c, sa, sb = c_ref[...], sa_ref[...], sb_ref[...]
    for h in range(N_HEADS):
        sl = slice(h * HEAD_DIM, (h + 1) * HEAD_DIM)
        a = acc[:, sl]
        o_ref[:, sl] = a * c + pltpu.roll(a, HEAD_DIM - ROPE_HALF, 1) * sa + pltpu.roll(a, ROPE_HALF, 1) * sb


def _gates_body(h_ref, w_ref, o_ref):
    o_ref[...] = jnp.dot(h_ref[...], w_ref[...], preferred_element_type=F32).astype(BF16)


def _rope_tables(S):
    inv_freq = ROPE_THETA ** (-jnp.arange(ROPE_HALF, dtype=F32) * 2.0 / ROPE_DIM)
    ang = jnp.arange(S, dtype=F32)[:, None] * inv_freq[None, :]
    cos, sin = jnp.cos(ang), jnp.sin(ang)
    pad = jnp.zeros((S, HEAD_DIM - ROPE_DIM), F32)
    zero = jnp.zeros((S, ROPE_HALF), F32)
    c = jnp.concatenate([cos, cos, pad + 1.0], axis=1)
    sa = jnp.concatenate([-sin, zero, pad], axis=1)
    sb = jnp.concatenate([zero, sin, pad], axis=1)
    scale = HEAD_DIM ** -0.5
    none = jnp.zeros_like(c)
    return (jnp.stack([c * scale, c, none + 1.0]), jnp.stack([sa * scale, sa, none]),
            jnp.stack([sb * scale, sb, none]))


def _in_projection(x2d, S, gain, w_bf16, tables):
    T = x2d.shape[0]
    tm = IN_TM
    pos_blocks = S // tm
    n_qkv = QKV_WIDTH // IN_TN
    x_spec = pl.BlockSpec((tm, D_MODEL), lambda i, j: (i, 0))
    g_spec = pl.BlockSpec((1, D_MODEL), lambda i, j: (0, 0))
    out_spec = pl.BlockSpec((tm, IN_TN), lambda i, j: (i, j))
    tab_spec = pl.BlockSpec((None, tm, HEAD_DIM), lambda i, j: (j, i % pos_blocks, 0))
    qkv, h = pl.pallas_call(
        _qkv_body,
        grid=(T // tm, n_qkv),
        in_specs=[x_spec, g_spec, pl.BlockSpec((D_MODEL, IN_TN), lambda i, j: (0, j)), tab_spec, tab_spec, tab_spec],
        out_specs=[out_spec, x_spec],
        out_shape=[jax.ShapeDtypeStruct((T, QKV_WIDTH), F32), jax.ShapeDtypeStruct((T, D_MODEL), BF16)],
        compiler_params=_params(("arbitrary", "arbitrary")),
        name="qkv_projection",
    )(x2d, gain, w_bf16, *tables)
    gates = pl.pallas_call(
        _gates_body,
        grid=(T // tm, GATE_WIDTH // IN_TN),
        in_specs=[x_spec, pl.BlockSpec((D_MODEL, IN_TN), lambda i, j: (0, n_qkv + j))],
        out_specs=out_spec,
        out_shape=jax.ShapeDtypeStruct((T, GATE_WIDTH), BF16),
        compiler_params=_params(("arbitrary", "arbitrary")),
        name="gate_projection",
    )(h, w_bf16)
    return qkv, gates


def _attn_configs(S):
    cfgs, tables = [], []
    for window, d in DILATED_PATTERNS:
        L = S // d
        qb = min(ATTN_QB, L)
        half = window // (2 * d)
        W = min(qb + 2 * half, L)
        assert L % qb == 0 and window % (2 * d) == 0
        assert (W == qb + 2 * half and L // qb >= 2) or (L == qb == W)
        if (qb, W, half) not in tables:
            tables.append((qb, W, half))
        cfgs.append((d, L, qb, W, half, tables.index((qb, W, half))))
    return cfgs, tables


def _mask_table(qb, W, half):
    rel = jnp.arange(W, dtype=jnp.int32)[None, :] - jnp.arange(qb, dtype=jnp.int32)[:, None]
    offs = jnp.array([half, 0, -half], jnp.int32)[:, None, None]
    u = rel[None] + offs
    return jnp.where((u >= 0) & (u <= 2 * half), 0.0, -1e30).astype(F32)


def _attn_body(*refs, S):
    cfgs, tables = _attn_configs(S)
    q_ref, k_ref, v_ref, gain_ref = refs[:4]
    mask_refs = refs[4:4 + len(tables)]
    o_ref, qc, kc, vc, og, lg, sc, mc, q32, k32, v32 = refs[4 + len(tables):]
    srcs, stages, f32s = (q_ref, k_ref, v_ref), (qc, kc, vc), (q32, k32, v32)
    d_prev = 1
    for g, (d, L, qb, W, half, tab) in enumerate(cfgs):
        n_blocks = L // qb
        mask_ref = mask_refs[tab]

        def rows_of_class(r, start, size, d=d):
            if d == 1:
                return pl.ds(start, size)
            return pl.ds(r + start * d, size, stride=d)

        from_prev = d_prev > 1 and d % d_prev == 0
        keep_f32 = any(c[0] > d and c[0] % d == 0 for c in cfgs) and d > 1
        step = d // d_prev if from_prev else d

        def stage(r, carry, d=d, L=L, qb=qb, from_prev=from_prev, keep_f32=keep_f32, step=step, d_prev=d_prev):
            dst = pl.ds(pl.multiple_of(r * L, qb), L)
            for src, st, f32 in zip(srcs, stages, f32s):
                if from_prev:
                    start = (r % d_prev) * (L * step) + r // d_prev
                    x = f32[pl.ds(start, L, stride=step), :]
                elif d == 1:
                    x = src[pl.ds(0, L), :]
                else:
                    x = src[pl.ds(r, L, stride=d), :]
                if keep_f32:
                    f32[dst, :] = x
                st[dst, :] = x.astype(BF16)
            return carry

        if d == 1:
            stage(0, 0)
        else:
            lax.fori_loop(0, d, stage, 0)
        d_prev = d

        def place(nb, L=L, qb=qb, half=half, W=W, n_blocks=n_blocks):
            r = nb // n_blocks
            q0 = pl.multiple_of((nb % n_blocks) * qb, qb)
            ws = jnp.clip(q0 - half, 0, L - W)
            case = jnp.where(q0 == 0, 0, jnp.where(q0 == L - qb, 2, 1))
            keys = pl.ds(pl.multiple_of(r * L + ws, BF16_SUBLANES), W)
            return r, q0, keys, case

        def scores(nb, c, g=g, L=L, qb=qb, W=W, mask_ref=mask_ref, place=place, rows_of_class=rows_of_class):
            r, q0, keys, case = place(nb)
            q = qc[pl.ds(pl.multiple_of(r * L + q0, qb), qb), :]
            s = lax.dot_general(q, kc[keys, :], (((1,), (1,)), ((), ())), preferred_element_type=F32)
            s = s + mask_ref[case]
            m = jnp.max(s, axis=-1, keepdims=True)
            sc[nb, 0:qb, 0:W] = s - m
            mc[pl.ds(pl.multiple_of(nb * qb, qb), qb), :] = jnp.broadcast_to(m, (qb, LANES))
            return c

        def values(nb, c, g=g, qb=qb, W=W, place=place, rows_of_class=rows_of_class):
            r, q0, keys, _ = place(nb)
            p = jnp.exp(sc[nb, 0:qb, 0:W])
            l = jnp.sum(p, axis=-1, keepdims=True)
            acc = jnp.dot(p.astype(BF16), vc[keys, :], preferred_element_type=F32)
            og[g, rows_of_class(r, q0, qb), :] = acc / l
            lg[g, rows_of_class(r, q0, qb), :] = mc[pl.ds(pl.multiple_of(nb * qb, qb), qb), :] + jnp.log(l)
            return c

        unroll = min(ATTN_UNROLL, S // qb)
        lax.fori_loop(0, S // qb, scores, 0, unroll=unroll)
        lax.fori_loop(0, S // qb, values, 0, unroll=unroll)

    def combine(c, carry):
        rows = pl.ds(pl.multiple_of(c * ATTN_FIN, ATTN_FIN), ATTN_FIN)
        l0, l1, l2 = lg[0, rows, :], lg[1, rows, :], lg[2, rows, :]
        mx = jnp.maximum(jnp.maximum(l0, l1), l2)
        e0, e1, e2 = jnp.exp(l0 - mx), jnp.exp(l1 - mx), jnp.exp(l2 - mx)
        a = (e0 * og[0, rows, :] + e1 * og[1, rows, :] + e2 * og[2, rows, :]) / (e0 + e1 + e2)
        ms = jnp.mean(a * a, axis=-1, keepdims=True)
        o_ref[rows, :] = (a * lax.rsqrt(ms + EPS) * gain_ref[...]).astype(BF16)
        return carry

    lax.fori_loop(0, S // ATTN_FIN, combine, 0, unroll=4)


def _band_attention(qkv, B, S, gain):
    cfgs, tables = _attn_configs(S)
    score_shape = (max(S // c[2] for c in cfgs), max(c[2] for c in cfgs), max(c[3] for c in cfgs))
    head = lambda t: pl.BlockSpec((S, HEAD_DIM), lambda b, h: (b, t * N_HEADS + h))
    masks = [_mask_table(*t) for t in tables]
    mask_specs = [pl.BlockSpec(m.shape, lambda b, h: (0, 0, 0)) for m in masks]
    return pl.pallas_call(
        functools.partial(_attn_body, S=S),
        grid=(B, N_HEADS),
        in_specs=[head(0), head(1), head(2), pl.BlockSpec((1, HEAD_DIM), lambda b, h: (0, h))] + mask_specs,
        out_specs=pl.BlockSpec((S, HEAD_DIM), lambda b, h: (b, h)),
        out_shape=jax.ShapeDtypeStruct((B * S, ATTN_WIDTH), BF16),
        scratch_shapes=[pltpu.VMEM((S, HEAD_DIM), BF16)] * 3
        + [pltpu.VMEM((len(DILATED_PATTERNS), S, HEAD_DIM), F32)] * 2
        + [pltpu.VMEM(score_shape, F32)] + [pltpu.VMEM((S, HEAD_DIM), F32)] * 4,
        compiler_params=_params(("arbitrary", "arbitrary")),
        name="band_attention",
    )(qkv, qkv, qkv, gain, *masks)


def _mix_body(attn_ref, cb_ref, cc_ref, cx_ref, ccp_ref, cxp_ref, ccn_ref, cxn_ref, x_ref, wout_ref,
              convw_ref, gc_ref, g2_ref, wrh_ref, wrb_ref, br_ref, tri_ref, base_ref,
              x1_ref, h2_ref, idx_ref, gate_ref, rank_ref, cnt_ref, conv_scr, cnt_scr, *, tm, S, n):
    s = pl.program_id(0)
    live = s >= 1
    buf_a = s % 2
    buf_b = 1 - buf_a

    @pl.when(s == 0)
    def _():
        conv_scr[1] = jnp.zeros((tm, CONV_WIDTH), BF16)
        cnt_scr[...] = base_ref[...]

    mix = (jnp.dot(attn_ref[...], wout_ref[0:ATTN_WIDTH, :], preferred_element_type=F32)
           + jnp.dot(conv_scr[buf_b], wout_ref[ATTN_WIDTH:D_MODEL, :], preferred_element_type=F32))
    x1 = x_ref[...] + mix
    x1_ref[...] = x1
    ms = jnp.mean(x1 * x1, axis=-1, keepdims=True)
    h2 = x1 * lax.rsqrt(ms + EPS) * g2_ref[...]
    hi = h2.astype(BF16)
    hi_f = hi.astype(F32)
    h2_ref[...] = hi_f

    lo = (h2 - hi_f).astype(BF16)
    hi_both = jnp.dot(hi, wrb_ref[...], preferred_element_type=F32)
    logits = (hi_both[:, :LANES] + hi_both[:, LANES:]
              + jnp.dot(lo, wrh_ref[...], preferred_element_type=F32)) + br_ref[...]

    lane = lax.broadcasted_iota(jnp.int32, (tm, LANES), 1)
    lane_f = lane.astype(F32)
    vals, ids = [], []
    for _ in range(TOP_K):
        m = jnp.max(logits, axis=-1, keepdims=True)
        ix = jnp.min(jnp.where(logits == m, lane_f, float(LANES)), axis=-1, keepdims=True)
        vals.append(m)
        ids.append(ix)
        logits = jnp.where(lane_f == ix, -jnp.inf, logits)
    es = [jnp.exp(v - vals[0]) for v in vals]
    den = es[0] + es[1] + es[2] + es[3]
    idx_tile = jnp.zeros((tm, LANES), F32)
    gate_tile = jnp.zeros((tm, LANES), F32)
    for k in range(TOP_K):
        idx_tile = jnp.where(lane == k, ids[k], idx_tile)
        gate_tile = jnp.where(lane == k, es[k] / den, gate_tile)
    idx_ref[...] = idx_tile.astype(jnp.int32)
    gate_ref[...] = gate_tile

    chosen = [lane_f == ids[k] for k in range(TOP_K)]
    picks = sum(c.astype(F32) for c in chosen) * live.astype(F32)
    before = cnt_scr[...] + jnp.dot(tri_ref[...], picks.astype(BF16), preferred_element_type=F32)
    rank_tile = jnp.zeros((tm, LANES), F32)
    for k in range(TOP_K):
        rk = jnp.sum(jnp.where(chosen[k], before, 0.0), axis=-1, keepdims=True)
        rank_tile = jnp.where(lane == k, rk, rank_tile)
    rank_ref[...] = rank_tile.astype(jnp.int32)
    cnt_scr[...] += jnp.sum(picks, axis=0, keepdims=True)
    cnt_ref[...] = cnt_scr[...]

    row = lax.broadcasted_iota(jnp.int32, (tm, LANES), 0)
    r0 = jnp.minimum(s, n - 1) * tm
    at_start = (r0 % S) == 0
    at_end = ((r0 + tm) % S) == 0
    last = BF16_SUBLANES - 1
    gw = CONV_WIDTH // N_CONV_GROUPS
    for c in range(N_CONV_GROUPS):
        sl = slice(c * gw, (c + 1) * gw)
        u = cc_ref[:, sl].astype(F32) * cx_ref[:, sl].astype(F32)
        up = ccp_ref[last:last + 1, sl].astype(F32) * cxp_ref[last:last + 1, sl].astype(F32)
        un = ccn_ref[0:1, sl].astype(F32) * cxn_ref[0:1, sl].astype(F32)
        up = jnp.where(at_start, 0.0, up)
        un = jnp.where(at_end, 0.0, un)
        u_prev = jnp.where(row == 0, up, pltpu.roll(u, 1, 0))
        u_next = jnp.where(row == tm - 1, un, pltpu.roll(u, tm - 1, 0))
        y = convw_ref[0:1, sl] * u_prev + convw_ref[1:2, sl] * u + convw_ref[2:3, sl] * u_next
        cv = cb_ref[:, sl].astype(F32) * y
        ms = jnp.mean(cv * cv, axis=-1, keepdims=True)
        conv_scr[buf_a, :, sl] = (cv * lax.rsqrt(ms + EPS) * gc_ref[:, sl]).astype(BF16)


def _mixer_out(x2d, S, attn, gates3, conv_w, gc, w_out, g2, wr_hi, wr_both, br, base_counts):
    T = x2d.shape[0]
    tm = MIX_TM
    tri = (jnp.arange(tm)[:, None] > jnp.arange(tm)[None, :]).astype(BF16)
    hr = BF16_SUBLANES
    n_halo = T // hr
    n = T // tm
    done = lambda s: jnp.maximum(s - 1, 0)
    conv = lambda s: jnp.minimum(s, n - 1)
    row_tile = lambda w: pl.BlockSpec((tm, w), lambda s: (done(s), 0))
    gate_tile = lambda c: pl.BlockSpec((tm, CONV_WIDTH), lambda s: (conv(s), c))
    prev_halo = lambda c: pl.BlockSpec(
        (hr, CONV_WIDTH), lambda s: (jnp.maximum(conv(s) * (tm // hr) - 1, 0), c))
    next_halo = lambda c: pl.BlockSpec(
        (hr, CONV_WIDTH), lambda s: (jnp.minimum((conv(s) + 1) * (tm // hr), n_halo - 1), c))
    const = lambda shape: pl.BlockSpec(shape, lambda s: (0, 0))
    cb_col, cc_col, cx_col = 0, 1, 2
    return pl.pallas_call(
        functools.partial(_mix_body, tm=tm, S=S, n=n),
        grid=(n + 1,),
        in_specs=[
            row_tile(ATTN_WIDTH),
            gate_tile(cb_col), gate_tile(cc_col), gate_tile(cx_col),
            prev_halo(cc_col), prev_halo(cx_col), next_halo(cc_col), next_halo(cx_col),
            row_tile(D_MODEL),
            const((D_MODEL, D_MODEL)),
            const((3, CONV_WIDTH)), const((1, CONV_WIDTH)), const((1, D_MODEL)),
            const((D_MODEL, LANES)), const((D_MODEL, 2 * LANES)), const((1, LANES)),
            const((tm, tm)), const((1, LANES)),
        ],
        out_specs=[row_tile(D_MODEL), row_tile(D_MODEL), row_tile(LANES), row_tile(LANES), row_tile(LANES),
                   const((1, LANES))],
        out_shape=[
            jax.ShapeDtypeStruct((T, D_MODEL), F32),
            jax.ShapeDtypeStruct((T, D_MODEL), F32),
            jax.ShapeDtypeStruct((T, LANES), jnp.int32),
            jax.ShapeDtypeStruct((T, LANES), F32),
            jax.ShapeDtypeStruct((T, LANES), jnp.int32),
            jax.ShapeDtypeStruct((1, LANES), F32),
        ],
        scratch_shapes=[pltpu.VMEM((2, tm, CONV_WIDTH), BF16), pltpu.VMEM((1, LANES), F32)],
        compiler_params=_params(("arbitrary",)),
        name="mixer_out_router",
    )(attn, gates3, gates3, gates3, gates3, gates3, gates3, gates3, x2d, w_out, conv_w,
      gc, g2, wr_hi, wr_both, br, tri, base_counts)


def _dispatch_body(nf_ref, slot_ref, ha_ref, hb_ref, xs_hbm, zero_scr, zsem, sem, *, steps_a):
    i = pl.program_id(0)
    n_sub = xs_hbm.shape[0] // MOE_SUB

    def zero_copy(sb):
        return pltpu.make_async_copy(zero_scr, xs_hbm.at[pl.ds(sb * MOE_SUB, MOE_SUB), :], zsem)

    @pl.when(i == 0)
    def _():
        zero_scr[...] = jnp.zeros_like(zero_scr)

        def start(sb, cnt):
            pl.when(nf_ref[sb] != 0)(lambda: zero_copy(sb).start())
            return cnt + nf_ref[sb]

        cnt = lax.fori_loop(0, n_sub, start, 0)

        def wait(_, c):
            zero_copy(0).wait()
            return c

        lax.fori_loop(0, cnt, wait, 0)

    def scatter(h_ref):
        for j in range(DISP_R):
            for k in range(TOP_K):
                s = slot_ref[0, j * TOP_K + k]
                pltpu.make_async_copy(h_ref.at[pl.ds(j, 1), :], xs_hbm.at[pl.ds(s, 1), :], sem).start(priority=k % 2)

    pl.when(i < steps_a)(lambda: scatter(ha_ref))
    pl.when(i >= steps_a)(lambda: scatter(hb_ref))
    n_rows = DISP_R * TOP_K
    pltpu.make_async_copy(xs_hbm.at[pl.ds(0, n_rows), :], xs_hbm.at[pl.ds(0, n_rows), :], sem).wait()


def _dispatch(h2_a, h2_b, slot, notfull, P):
    steps_a = h2_a.shape[0] // DISP_R
    steps_b = h2_b.shape[0] // DISP_R
    n_steps = steps_a + steps_b
    grid_spec = pltpu.PrefetchScalarGridSpec(
        num_scalar_prefetch=1,
        grid=(n_steps,),
        in_specs=[
            pl.BlockSpec((None, 1, DISP_R * TOP_K), lambda i, nf: (i, 0, 0), memory_space=pltpu.SMEM),
            pl.BlockSpec((DISP_R, D_MODEL), lambda i, nf: (jnp.minimum(i, steps_a - 1), 0)),
            pl.BlockSpec((DISP_R, D_MODEL), lambda i, nf: (jnp.maximum(i - steps_a, 0), 0)),
        ],
        out_specs=pl.BlockSpec(memory_space=pl.ANY),
        scratch_shapes=[pltpu.VMEM((MOE_SUB, D_MODEL), F32), pltpu.SemaphoreType.DMA(()),
                        pltpu.SemaphoreType.DMA(())],
    )
    return pl.pallas_call(
        functools.partial(_dispatch_body, steps_a=steps_a),
        grid_spec=grid_spec,
        out_shape=jax.ShapeDtypeStruct((P, D_MODEL), F32),
        compiler_params=_params(("arbitrary",)),
        name="dispatch",
    )(notfull, slot.reshape(n_steps, 1, DISP_R * TOP_K), h2_a, h2_b)


def _expert_body(te_ref, tx_ref, ns_ref, first_ref, x_ref, bg_ref, bu_ref, bd_ref, wg_hbm, wu_hbm, wd_hbm,
                 y_ref, xb, wg_res, wu_res, wd_res, stg_g, stg_u, stg_d, sems):
    i = pl.program_id(0)
    e = te_ref[i]
    nsub = ns_ref[i]
    first = first_ref[i]
    nf = D_FF // MOE_TF
    kh = D_MODEL // 2
    fh = MOE_TF // 2

    full_later_tile = (first == 0) & (nsub == MOE_NSUB)

    @pl.when(jnp.logical_not(full_later_tile))
    def _():
        for s in range(MOE_NSUB):
            rows = slice(s * MOE_SUB, (s + 1) * MOE_SUB)
            y_ref[rows, :] = jnp.where(s < nsub, jnp.broadcast_to(bd_ref[...], (MOE_SUB, D_MODEL)), 0.0)

    @pl.when((nsub > 0) & jnp.logical_not(full_later_tile))
    def _():
        xb[...] = x_ref[...].astype(BF16)

    def sub_block(s, f, from_bias=False, n_sub=1):
        rows = slice(s * MOE_SUB, (s + n_sub) * MOE_SUB)
        x = xb[rows, :]
        g = jnp.dot(x, wg_res[f], preferred_element_type=F32) + bg_ref[f]
        u = jnp.dot(x, wu_res[f], preferred_element_type=F32) + bu_ref[f]
        g = jnp.minimum(g, SWIGLU_LIMIT)
        u = jnp.clip(u, -SWIGLU_LIMIT, SWIGLU_LIMIT)
        a = (u + 1.0) * (g * jax.nn.sigmoid(SWIGLU_ALPHA * g))
        dn = jnp.dot(a.astype(BF16), wd_res[f], preferred_element_type=F32)
        if from_bias:
            y_ref[rows, :] = bd_ref[...] + dn
        else:
            y_ref[rows, :] += dn

    def half_copies(f, h):
        cols = pl.ds(f * MOE_TF, MOE_TF)
        drows = pl.ds(f * MOE_TF + h * fh, fh)
        return (pltpu.make_async_copy(wg_hbm.at[e, pl.ds(h * kh, kh), cols], stg_g.at[h], sems.at[0, h]),
                pltpu.make_async_copy(wu_hbm.at[e, pl.ds(h * kh, kh), cols], stg_u.at[h], sems.at[1, h]),
                pltpu.make_async_copy(wd_hbm.at[e, drows, :], stg_d.at[h], sems.at[2, h]))

    @pl.when(first == 1)
    def _():
        for h in range(2):
            for c in half_copies(0, h):
                c.start()

        def load_and_compute(f, carry):
            for h in range(2):
                for c in half_copies(f, h):
                    c.wait()
                wg_res[f, h * kh:(h + 1) * kh, :] = stg_g[h].astype(BF16)
                wu_res[f, h * kh:(h + 1) * kh, :] = stg_u[h].astype(BF16)
                wd_res[f, h * fh:(h + 1) * fh, :] = stg_d[h].astype(BF16)

                @pl.when(f + 1 < nf)
                def _():
                    for c in half_copies(f + 1, h):
                        c.start()
            pl.when(nsub == MOE_NSUB)(functools.partial(sub_block, 0, f, n_sub=MOE_NSUB))
            for s in range(MOE_NSUB - 1):
                pl.when((s < nsub) & (nsub < MOE_NSUB))(functools.partial(sub_block, s, f))
            return carry

        lax.fori_loop(0, nf, load_and_compute, 0)

    @pl.when(full_later_tile)
    def _():
        xb[...] = x_ref[...].astype(BF16)
        sub_block(0, 0, from_bias=True, n_sub=MOE_NSUB)

        def full(f, carry):
            sub_block(0, f, n_sub=MOE_NSUB)
            return carry
        lax.fori_loop(1, nf, full, 0, unroll=4)

    @pl.when((first == 0) & (nsub > 0) & (nsub < MOE_NSUB))
    def _():
        def part(f, carry):
            for s in range(MOE_NSUB - 1):
                pl.when(s < nsub)(functools.partial(sub_block, s, f))
            return carry
        lax.fori_loop(0, nf, part, 0)


def _expert_ffn(xs, tile_expert, tile_xblk, tile_nsub, tile_first, w_gate, b_gate, w_up, b_up, w_down, b_down):
    P = xs.shape[0]
    n_tiles = P // MOE_TILE
    nf = D_FF // MOE_TF
    per_expert = lambda shape: pl.BlockSpec(
        (None,) + shape, lambda i, te, tx, ns, fi: (te[i],) + (0,) * len(shape))
    grid_spec = pltpu.PrefetchScalarGridSpec(
        num_scalar_prefetch=4,
        grid=(n_tiles,),
        in_specs=[
            pl.BlockSpec((MOE_TILE, D_MODEL), lambda i, te, tx, ns, fi: (tx[i], 0)),
            per_expert((nf, 1, MOE_TF)), per_expert((nf, 1, MOE_TF)), per_expert((1, D_MODEL)),
            pl.BlockSpec(memory_space=pl.ANY), pl.BlockSpec(memory_space=pl.ANY), pl.BlockSpec(memory_space=pl.ANY),
        ],
        out_specs=pl.BlockSpec((MOE_TILE, D_MODEL), lambda i, te, tx, ns, fi: (i, 0)),
        scratch_shapes=[
            pltpu.VMEM((MOE_TILE, D_MODEL), BF16),
            pltpu.VMEM((nf, D_MODEL, MOE_TF), BF16), pltpu.VMEM((nf, D_MODEL, MOE_TF), BF16),
            pltpu.VMEM((nf, MOE_TF, D_MODEL), BF16),
            pltpu.VMEM((2, D_MODEL // 2, MOE_TF), F32), pltpu.VMEM((2, D_MODEL // 2, MOE_TF), F32),
            pltpu.VMEM((2, MOE_TF // 2, D_MODEL), F32),
            pltpu.SemaphoreType.DMA((3, 2)),
        ],
    )
    return pl.pallas_call(
        _expert_body,
        grid_spec=grid_spec,
        out_shape=jax.ShapeDtypeStruct((P, D_MODEL), F32),
        compiler_params=_params(("arbitrary",)),
        name="expert_ffn",
    )(tile_expert, tile_xblk, tile_nsub, tile_first, xs,
      b_gate.reshape(N_EXPERTS, nf, 1, MOE_TF), b_up.reshape(N_EXPERTS, nf, 1, MOE_TF),
      b_down.reshape(N_EXPERTS, 1, D_MODEL), w_gate, w_up, w_down)


def _final_body(slot_ref, slot_next_ref, x1_ref, gate_ref, gf_ref, y_hbm, o_ref, ybuf, sems, *, n):
    i = pl.program_id(0)
    tm = FIN_TM

    def gather(s_ref, b):
        for j in range(tm):
            for k in range(TOP_K):
                s = s_ref[0, j * TOP_K + k]
                pltpu.make_async_copy(y_hbm.at[pl.ds(s, 1), :], ybuf.at[b, k, pl.ds(j, 1), :],
                                      sems.at[b]).start(priority=k % 2)

    b = i % 2
    pl.when(i == 0)(lambda: gather(slot_ref, 0))
    pl.when((i + 1 < n) & (b == 0))(lambda: gather(slot_next_ref, 1))
    pl.when((i + 1 < n) & (b == 1))(lambda: gather(slot_next_ref, 0))
    pltpu.make_async_copy(ybuf.at[b], ybuf.at[b], sems.at[b]).wait()

    gates = gate_ref[...]
    acc = x1_ref[...]
    for k in range(TOP_K):
        acc = acc + gates[:, k:k + 1] * ybuf[b, k]
    ms = jnp.mean(acc * acc, axis=-1, keepdims=True)
    o_ref[...] = acc * lax.rsqrt(ms + EPS) * gf_ref[...]


def _combine_final(x1, gates, slot, y, gf):
    T = x1.shape[0]
    tm = FIN_TM
    n_steps = T // tm
    row_tile = lambda w: pl.BlockSpec((tm, w), lambda i: (i, 0))
    slot3 = slot.reshape(n_steps, 1, tm * TOP_K)
    slot_spec = lambda step: pl.BlockSpec((None, 1, tm * TOP_K), step, memory_space=pltpu.SMEM)
    return pl.pallas_call(
        functools.partial(_final_body, n=n_steps),
        grid=(n_steps,),
        in_specs=[
            slot_spec(lambda i: (i, 0, 0)),
            slot_spec(lambda i: (jnp.minimum(i + 1, n_steps - 1), 0, 0)),
            row_tile(D_MODEL), row_tile(LANES),
            pl.BlockSpec((1, D_MODEL), lambda i: (0, 0)),
            pl.BlockSpec(memory_space=pl.ANY),
        ],
        out_specs=row_tile(D_MODEL),
        out_shape=jax.ShapeDtypeStruct((T, D_MODEL), F32),
        scratch_shapes=[pltpu.VMEM((2, TOP_K, tm, D_MODEL), F32), pltpu.SemaphoreType.DMA((2,))],
        compiler_params=_params(("arbitrary",)),
        name="combine_final_norm",
    )(slot3, slot3, x1, gates, gf, y)


def _route(idx, rank, counts):
    n = idx.shape[0] * TOP_K
    n_tiles = n // MOE_TILE + N_EXPERTS
    tiles_per_e = (counts + MOE_TILE - 1) // MOE_TILE
    tile_end = jnp.cumsum(tiles_per_e)
    tile_start = tile_end - tiles_per_e
    experts = jnp.arange(N_EXPERTS, dtype=jnp.int32)
    region = jnp.sum(jnp.where(idx[..., None] == experts, tile_start * MOE_TILE, 0), axis=-1)
    slot = region + rank

    t = jnp.arange(n_tiles, dtype=jnp.int32)
    used = t < tile_end[-1]
    last_used = jnp.maximum(tile_end[-1] - 1, 0)
    tt = jnp.where(used, t, last_used)
    te = jnp.minimum(jnp.sum(tt[:, None] >= tile_end[None, :], axis=1), N_EXPERTS - 1).astype(jnp.int32)
    rows = jnp.where(used, jnp.clip(counts[te] - (tt - tile_start[te]) * MOE_TILE, 0, MOE_TILE), 0)
    nsub = ((rows + MOE_SUB - 1) // MOE_SUB).astype(jnp.int32)
    sub_end = (jnp.arange(MOE_NSUB, dtype=jnp.int32) + 1) * MOE_SUB
    notfull = (rows[:, None] < sub_end[None, :]).astype(jnp.int32).reshape(-1)
    first = (used & (t == tile_start[te])).astype(jnp.int32)
    return slot, notfull, te, tt.astype(jnp.int32), nsub, first, n_tiles * MOE_TILE


def kernel(x_prompt, x_sample, norm1_g, w_in, conv_w, attn_norm_g, conv_norm_g, w_out, norm2_g,
           w_router, b_router, w_gate, b_gate, w_up, b_up, w_down, b_down, final_norm_g):
    assert norm1_g.shape[0] == 1, "one layer"
    w_in_b = w_in[0].astype(BF16)
    w_out_b = w_out[0].astype(BF16)
    wr = jnp.pad(w_router[0], ((0, 0), (0, LANES - N_EXPERTS)))
    wr_hi = wr.astype(BF16)
    wr_both = jnp.concatenate([wr_hi, (wr - wr_hi.astype(F32)).astype(BF16)], axis=1)
    br = jnp.pad(b_router[0], (0, LANES - N_EXPERTS), constant_values=-jnp.inf).reshape(1, LANES)
    g1 = norm1_g[0].reshape(1, D_MODEL)
    g2 = norm2_g[0].reshape(1, D_MODEL)
    ga = attn_norm_g[0].reshape(1, ATTN_WIDTH)
    gc = conv_norm_g[0].reshape(1, CONV_WIDTH)
    gf = final_norm_g.reshape(1, D_MODEL)

    x1s, h2s, idxs, ranks, gates = [], [], [], [], []
    counts = jnp.zeros((1, LANES), F32)
    for x in (x_prompt, x_sample):
        B, S, _ = x.shape
        x2d = x.reshape(B * S, D_MODEL)
        qkv, gates3 = _in_projection(x2d, S, g1, w_in_b, _rope_tables(S))
        attn = _band_attention(qkv, B, S, ga)
        x1, h2, idx, gate, rank, counts = _mixer_out(
            x2d, S, attn, gates3, conv_w[0], gc, w_out_b, g2, wr_hi, wr_both, br, counts)
        x1s.append(x1)
        h2s.append(h2)
        idxs.append(idx[:, :TOP_K])
        ranks.append(rank[:, :TOP_K])
        gates.append(gate)

    slot, notfull, te, tx, nsub, first, P = _route(
        jnp.concatenate(idxs, axis=0), jnp.concatenate(ranks, axis=0), counts[0, :N_EXPERTS].astype(jnp.int32))
    slots = []
    t0 = 0
    for h2 in h2s:
        slots.append(slot[t0:t0 + h2.shape[0]])
        t0 += h2.shape[0]
    xs = _dispatch(h2s[0], h2s[1], slot, notfull, P)
    y = _expert_ffn(xs, te, tx, nsub, first, w_gate[0], b_gate[0], w_up[0], b_up[0], w_down[0], b_down[0])
    return tuple(_combine_final(x1, gate, sl, y, gf).reshape(x.shape)
                 for x, x1, gate, sl in zip((x_prompt, x_sample), x1s, gates, slots))
```
